```python
import math
import jax
import jax.numpy as jnp
from jax import lax
import numpy as np

D_MODEL = 1024
BATCH = 2
SEQ = 16384
DEPTH = 2

N_EVEN = (DEPTH + 1) // 2
N_ODD = DEPTH // 2

HGRN_HEADS = 4
HGRN_DK = 128
HGRN_DV = 128
HGRN_QK_WIDTH = HGRN_HEADS * HGRN_DK
HGRN_V_WIDTH = HGRN_HEADS * HGRN_DV
HGRN_CHUNK = 32
S5_WIDTH = D_MODEL // 2
S5_GROUP = 16
S5_GROUPS = S5_WIDTH // S5_GROUP
S5_STATE = 64
EVEN_PROJ = 2 * HGRN_QK_WIDTH + 2 * HGRN_V_WIDTH + S5_WIDTH
EVEN_MIX = HGRN_V_WIDTH + S5_WIDTH
GDN_HEADS = 8
GDN_DK = 128
GDN_DV = 128
GDN_QK = GDN_HEADS * GDN_DK
GDN_V = GDN_HEADS * GDN_DV
GDN_CONV = 4
GDN_CHUNK = 64
ODD_PROJ = 2 * GDN_QK + 2 * GDN_V + 2 * GDN_HEADS
PEER_HEADS = 8
PEER_NKEYS = 128
PEER_EXPERTS = PEER_NKEYS * PEER_NKEYS
PEER_DKEY = 256
PEER_DHALF = PEER_DKEY // 2
PEER_TOPK = 16
PEER_BLOCK = 128

RMS_EPS = 1e-6
L2_EPS = 1e-6

kernel_name = 'hybrid_hgrn2_s5_gdn_peer'


def rmsnorm(x, w):
    x32 = x.astype(jnp.float32)
    y = x32 * lax.rsqrt(jnp.mean(x32 * x32, axis=-1, keepdims=True) + RMS_EPS)
    return (y * w.astype(jnp.float32)).astype(x.dtype)


def l2norm(x):
    return x * lax.rsqrt(jnp.sum(x * x, axis=-1, keepdims=True) + L2_EPS)


def to_chunks(t, c):
    b, s, h, d = t.shape
    return t.reshape(b, s // c, c, h, d).transpose(1, 0, 3, 2, 4)


def from_chunks(t):
    n, b, h, c, d = t.shape
    return t.transpose(1, 0, 3, 2, 4).reshape(b, n * c, h, d)


def scalar_chunks(t, c):
    b, s, h = t.shape
    return t.reshape(b, s // c, c, h).transpose(1, 0, 3, 2)


def hgrn2_mixer(q_raw, f_raw, i_raw, g_raw, lower_bound, gn_w):
    bsz, s, _ = q_raw.shape
    c = HGRN_CHUNK
    hk = (bsz, s, HGRN_HEADS, HGRN_DK)
    hv = (bsz, s, HGRN_HEADS, HGRN_DV)
    lb = lower_bound.astype(jnp.float32).reshape(HGRN_HEADS, HGRN_DK)
    q = jax.nn.silu(q_raw.reshape(hk)) * (HGRN_DK ** -0.5)
    f = lb + (1.0 - lb) * jax.nn.sigmoid(f_raw.reshape(hk))
    qc = to_chunks(q, c)
    kc = to_chunks(1.0 - f, c)
    vc = to_chunks(i_raw.reshape(hv), c)
    cum = jnp.cumsum(to_chunks(jnp.log(f), c), axis=3)
    cum_last = cum[:, :, :, -1:, :]
    q_e = qc * jnp.exp(cum)
    k_e = kc * jnp.exp(-cum)
    k_dec = kc * jnp.exp(cum_last - cum)
    tot = jnp.exp(cum_last[:, :, :, 0, :])
    causal = jnp.tril(jnp.ones((c, c), dtype=bool))
    att = jnp.where(causal, jnp.einsum('nbhtd,nbhsd->nbhts', q_e, k_e), 0.0)
    o_intra = jnp.einsum('nbhts,nbhse->nbhte', att, vc)

    def step(state, inp):
        q_n, k_n, v_n, tot_n = inp
        o_n = jnp.einsum('bhtd,bhde->bhte', q_n, state)
        state = tot_n[..., None] * state + jnp.einsum('bhtd,bhte->bhde', k_n, v_n)
        return state, o_n

    s0 = jnp.zeros((bsz, HGRN_HEADS, HGRN_DK, HGRN_DV), jnp.float32)
    _, o_inter = lax.scan(step, s0, (q_e, k_dec, vc, tot))
    o = from_chunks(o_intra + o_inter)
    o = rmsnorm(o, gn_w) * jax.nn.silu(g_raw.reshape(hv))
    return o.reshape(bsz, s, HGRN_V_WIDTH)


def _complex_affine_combine(e1, e2):
    a1r, a1i, b1r, b1i = e1
    a2r, a2i, b2r, b2i = e2
    ar = a1r * a2r - a1i * a2i
    ai = a1r * a2i + a1i * a2r
    br = a2r * b1r - a2i * b1i + b2r
    bi = a2r * b1i + a2i * b1r + b2i
    return (ar, ai, br, bi)


def s5_mixer(u, a_re, a_im, log_dt, b_re, b_im, c_re, c_im, d_skip, glu_w):
    f32 = jnp.float32
    bsz, s, _ = u.shape
    a_re, a_im, b_re, b_im, c_re, c_im = (t.astype(f32) for t in (a_re, a_im, b_re, b_im, c_re, c_im))
    ug = u.reshape(bsz, s, S5_GROUPS, S5_GROUP)
    dt = jnp.exp(log_dt.astype(f32))[:, None]
    mag = jnp.exp(a_re * dt)
    abar_re = mag * jnp.cos(a_im * dt)
    abar_im = mag * jnp.sin(a_im * dt)
    den = a_re * a_re + a_im * a_im
    coef_re = ((abar_re - 1.0) * a_re + abar_im * a_im) / den
    coef_im = (abar_im * a_re - (abar_re - 1.0) * a_im) / den
    bb_re = coef_re[..., None] * b_re - coef_im[..., None] * b_im
    bb_im = coef_re[..., None] * b_im + coef_im[..., None] * b_re
    bu_re = jnp.einsum('bsgp,gnp->bsgn', ug, bb_re)
    bu_im = jnp.einsum('bsgp,gnp->bsgn', ug, bb_im)
    shape = (1, s, S5_GROUPS, S5_STATE)
    elems = (jnp.broadcast_to(abar_re, shape), jnp.broadcast_to(abar_im, shape), bu_re, bu_im)
    _, _, x_re, x_im = lax.associative_scan(_complex_affine_combine, elems, axis=1)
    y = (jnp.einsum('bsgn,gpn->bsgp', x_re, c_re) - jnp.einsum('bsgn,gpn->bsgp', x_im, c_im)
         + d_skip.astype(f32) * ug)
    y = jax.nn.gelu(y.reshape(bsz, s, S5_WIDTH), approximate=False)
    return y * jax.nn.sigmoid(y @ glu_w.astype(f32))


def gated_deltanet_mixer(qkv_raw, z_raw, beta_raw, a_raw, conv_w, a_log, dt_bias, gn_w):
    f32 = jnp.float32
    bsz, s, ch = qkv_raw.shape
    c = GDN_CHUNK
    qkv = lax.conv_general_dilated(qkv_raw, conv_w.astype(f32)[:, None, :], window_strides=(1,),
                                   padding=[(GDN_CONV - 1, 0)], dimension_numbers=('NWC', 'WIO', 'NWC'),
                                   feature_group_count=ch)
    qkv = jax.nn.silu(qkv)
    q, k, v = jnp.split(qkv, [GDN_QK, 2 * GDN_QK], axis=-1)
    q = l2norm(q.reshape(bsz, s, GDN_HEADS, GDN_DK)) * (GDN_DK ** -0.5)
    k = l2norm(k.reshape(bsz, s, GDN_HEADS, GDN_DK))
    v = v.reshape(bsz, s, GDN_HEADS, GDN_DV)
    beta = jax.nn.sigmoid(beta_raw)
    g = -jnp.exp(a_log.astype(f32)) * jax.nn.softplus(a_raw + dt_bias.astype(f32))
    qc, kc, vc = to_chunks(q, c), to_chunks(k, c), to_chunks(v, c)
    bc = scalar_chunks(beta, c)[..., None]
    decay = jnp.cumsum(scalar_chunks(g, c), axis=-1)
    incl = jnp.tril(jnp.ones((c, c), dtype=bool))
    strict = jnp.tril(jnp.ones((c, c), dtype=bool), -1)
    diff = decay[..., :, None] - decay[..., None, :]
    lmask = jnp.where(incl, jnp.exp(jnp.where(incl, diff, 0.0)), 0.0)
    kb = kc * bc
    a_mat = jnp.where(strict, jnp.einsum('nbhtd,nbhsd->nbhts', kb, kc) * lmask, 0.0)
    eye = jnp.eye(c, dtype=a_mat.dtype)
    t_inv = lax.linalg.triangular_solve(a_mat + eye, jnp.broadcast_to(eye, a_mat.shape),
                                        left_side=True, lower=True)
    value = jnp.einsum('nbhts,nbhse->nbhte', t_inv, vc * bc)
    k_cum = jnp.einsum('nbhts,nbhsd->nbhtd', t_inv, kb * jnp.exp(decay)[..., None])
    attn = jnp.einsum('nbhtd,nbhsd->nbhts', qc, kc) * lmask
    q_dec = qc * jnp.exp(decay)[..., None]
    k_tail = kc * jnp.exp(decay[..., -1:] - decay)[..., None]
    tot = jnp.exp(decay[..., -1])

    def step(state, inp):
        value_n, kcum_n, attn_n, qdec_n, ktail_n, tot_n = inp
        v_new = value_n - jnp.einsum('bhtd,bhde->bhte', kcum_n, state)
        o_n = jnp.einsum('bhtd,bhde->bhte', qdec_n, state) + jnp.einsum('bhts,bhse->bhte', attn_n, v_new)
        state = tot_n[..., None, None] * state + jnp.einsum('bhtd,bhte->bhde', ktail_n, v_new)
        return state, o_n

    s0 = jnp.zeros((bsz, GDN_HEADS, GDN_DK, GDN_DV), f32)
    _, o = lax.scan(step, s0, (value, k_cum, attn, q_dec, k_tail, tot))
    o = from_chunks(o)
    o = rmsnorm(o, gn_w) * jax.nn.silu(z_raw.reshape(bsz, s, GDN_HEADS, GDN_DV))
    return o.reshape(bsz, s, GDN_V)


def peer_ffn(xn, w_q, sub_keys, u_tab, v_tab):
    bsz, s, d = xn.shape
    n_tok = bsz * s
    xt = xn.reshape(n_tok, d)
    q = (xt @ w_q).astype(jnp.float32).reshape(n_tok, PEER_HEADS, 2, PEER_DHALF)
    scores = jnp.einsum('thcd,ckd->thck', q, sub_keys.astype(jnp.float32))
    top_s, top_i = lax.top_k(scores, PEER_TOPK)
    n_cand = PEER_TOPK * PEER_TOPK
    cand_s = (top_s[:, :, 0, :, None] + top_s[:, :, 1, None, :]).reshape(n_tok, PEER_HEADS, n_cand)
    cand_e = (top_i[:, :, 0, :, None] * PEER_NKEYS + top_i[:, :, 1, None, :]).reshape(n_tok, PEER_HEADS, n_cand)
    sel_s, sel_pos = lax.top_k(cand_s, PEER_TOPK)
    expert_id = jnp.take_along_axis(cand_e, sel_pos, axis=-1).reshape(n_tok, PEER_HEADS * PEER_TOPK)
    gate = jax.nn.softmax(sel_s, axis=-1).reshape(n_tok, PEER_HEADS * PEER_TOPK)
    nb = n_tok // PEER_BLOCK

    def block(args):
        xb, eb, gb = args
        act = jax.nn.gelu(jnp.einsum('td,tkd->tk', xb, u_tab[eb]).astype(jnp.float32), approximate=False)
        return jnp.einsum('tk,tkd->td', (gb * act).astype(xb.dtype), v_tab[eb])

    out = lax.map(block, (xt.reshape(nb, PEER_BLOCK, d),
                          expert_id.reshape(nb, PEER_BLOCK, PEER_HEADS * PEER_TOPK),
                          gate.reshape(nb, PEER_BLOCK, PEER_HEADS * PEER_TOPK)))
    return out.reshape(bsz, s, d)


def setup_inputs(seed: int = 0) -> dict:
    key = jax.random.key(seed)
    k = jax.random.split(key, 27)
    f32 = jnp.float32

    def nrm(kk, shape, scale):
        return scale * jax.random.normal(kk, shape, f32)

    state_idx = jnp.arange(S5_STATE, dtype=f32)
    dt_gdn = jnp.exp(jax.random.uniform(k[21], (N_ODD, GDN_HEADS), f32, math.log(1e-3), math.log(1e-1)))
    return {
        'x': nrm(k[0], (BATCH, SEQ, D_MODEL), 1.0),
        'norm_mix_w': 1.0 + nrm(k[1], (DEPTH, D_MODEL), 0.02),
        'norm_ffn_w': 1.0 + nrm(k[2], (DEPTH, D_MODEL), 0.02),
        'norm_out_w': 1.0 + nrm(k[3], (D_MODEL,), 0.02),
        'ev_in_w': nrm(k[4], (N_EVEN, D_MODEL, EVEN_PROJ), D_MODEL ** -0.5),
        'ev_out_w': nrm(k[5], (N_EVEN, EVEN_MIX, D_MODEL), EVEN_MIX ** -0.5),
        'hgrn_lb': nrm(k[6], (N_EVEN + 1, HGRN_QK_WIDTH), 0.1),
        'hgrn_gn_w': 1.0 + nrm(k[7], (N_EVEN, HGRN_DV), 0.02),
        's5_a_re': -0.5 + nrm(k[8], (N_EVEN, S5_GROUPS, S5_STATE), 0.01),
        's5_a_im': jnp.pi * state_idx + nrm(k[9], (N_EVEN, S5_GROUPS, S5_STATE), 0.01),
        's5_log_dt': jax.random.uniform(k[10], (N_EVEN, S5_GROUPS), f32, math.log(1e-3), math.log(1e-1)),
        's5_b_re': nrm(k[11], (N_EVEN, S5_GROUPS, S5_STATE, S5_GROUP), (2 * S5_GROUP) ** -0.5),
        's5_b_im': nrm(k[12], (N_EVEN, S5_GROUPS, S5_STATE, S5_GROUP), (2 * S5_GROUP) ** -0.5),
        's5_c_re': nrm(k[13], (N_EVEN, S5_GROUPS, S5_GROUP, S5_STATE), 0.5),
        's5_c_im': nrm(k[14], (N_EVEN, S5_GROUPS, S5_GROUP, S5_STATE), 0.5),
        's5_d': nrm(k[15], (N_EVEN, S5_GROUPS, S5_GROUP), 1.0),
        's5_glu_w': nrm(k[16], (N_EVEN, S5_WIDTH, S5_WIDTH), S5_WIDTH ** -0.5),
        'od_in_w': nrm(k[17], (N_ODD, D_MODEL, ODD_PROJ), D_MODEL ** -0.5),
        'od_out_w': nrm(k[18], (N_ODD, GDN_V, D_MODEL), GDN_V ** -0.5),
        'gdn_conv_w': nrm(k[19], (N_ODD, GDN_CONV, 2 * GDN_QK + GDN_V), GDN_CONV ** -0.5),
        'gdn_a_log': jnp.log(jax.random.uniform(k[20], (N_ODD, GDN_HEADS), f32, 1.0, 16.0)),
        'gdn_dt_bias': dt_gdn + jnp.log(-jnp.expm1(-dt_gdn)),
        'gdn_gn_w': 1.0 + nrm(k[22], (N_ODD, GDN_DV), 0.02),
        'peer_wq': nrm(k[23], (DEPTH, D_MODEL, PEER_HEADS * PEER_DKEY), D_MODEL ** -0.5),
        'peer_sub_keys': nrm(k[24], (DEPTH, 2, PEER_NKEYS, PEER_DHALF), PEER_DHALF ** -0.5),
        'peer_u': nrm(k[25], (DEPTH, PEER_EXPERTS, D_MODEL), D_MODEL ** -0.5),
        'peer_v': nrm(k[26], (DEPTH, PEER_EXPERTS, D_MODEL), PEER_HEADS ** -0.5),
    }


def reference(x, norm_mix_w, norm_ffn_w, norm_out_w, ev_in_w, ev_out_w, hgrn_lb, hgrn_gn_w,
              s5_a_re, s5_a_im, s5_log_dt, s5_b_re, s5_b_im, s5_c_re, s5_c_im, s5_d, s5_glu_w,
              od_in_w, od_out_w, gdn_conv_w, gdn_a_log, gdn_dt_bias, gdn_gn_w,
              peer_wq, peer_sub_keys, peer_u, peer_v):
    lb_all = jnp.cumsum(jax.nn.softmax(hgrn_lb.astype(jnp.float32), axis=0), axis=0)
    even_splits = [HGRN_QK_WIDTH, 2 * HGRN_QK_WIDTH, 2 * HGRN_QK_WIDTH + HGRN_V_WIDTH,
                   2 * HGRN_QK_WIDTH + 2 * HGRN_V_WIDTH]
    odd_splits = [2 * GDN_QK + GDN_V, 2 * GDN_QK + 2 * GDN_V, 2 * GDN_QK + 2 * GDN_V + GDN_HEADS]
    h = x
    for layer in range(DEPTH):
        j = layer // 2
        hn = rmsnorm(h, norm_mix_w[layer])
        if layer % 2 == 0:
            proj = (hn @ ev_in_w[j]).astype(jnp.float32)
            q_raw, f_raw, i_raw, g_raw, u_raw = jnp.split(proj, even_splits, axis=-1)
            o_a = hgrn2_mixer(q_raw, f_raw, i_raw, g_raw, lb_all[j], hgrn_gn_w[j])
            o_b = s5_mixer(u_raw, s5_a_re[j], s5_a_im[j], s5_log_dt[j], s5_b_re[j], s5_b_im[j],
                           s5_c_re[j], s5_c_im[j], s5_d[j], s5_glu_w[j])
            mixed = jnp.concatenate([o_a, o_b], axis=-1).astype(h.dtype) @ ev_out_w[j]
        else:
            proj = (hn @ od_in_w[j]).astype(jnp.float32)
            qkv_raw, z_raw, beta_raw, a_raw = jnp.split(proj, odd_splits, axis=-1)
            o_c = gated_deltanet_mixer(qkv_raw, z_raw, beta_raw, a_raw, gdn_conv_w[j], gdn_a_log[j],
                                       gdn_dt_bias[j], gdn_gn_w[j])
            mixed = o_c.astype(h.dtype) @ od_out_w[j]
        h = h + mixed
        h = h + peer_ffn(rmsnorm(h, norm_ffn_w[layer]), peer_wq[layer], peer_sub_keys[layer],
                         peer_u[layer], peer_v[layer])
    return rmsnorm(h, norm_out_w)
```

```python
import functools
import math

import jax
import jax.numpy as jnp
from jax import lax
from jax.experimental import pallas as pl
from jax.experimental.pallas import tpu as pltpu

F32 = jnp.float32
BF16 = jnp.bfloat16
HI = lax.Precision.HIGHEST

RMS_EPS = 1e-6
L2_EPS = 1e-6
LANES = 128
VMEM_LIMIT = 56 * 1024 * 1024

HGRN_HEADS, HGRN_D, HGRN_CHUNK = 4, 128, 32
S5_GROUP, S5_STATE, S5_L = 16, 64, 8
S5_GB = LANES // S5_GROUP
GDN_HEADS, GDN_D, GDN_CONV, GDN_CHUNK = 8, 128, 4, 64
PEER_HEADS, PEER_NKEYS, PEER_DHALF, PEER_TOPK = 8, 128, 128, 16
MASKED = 1e30


def _dot(a, b, precision=None):
    return jnp.dot(a, b, preferred_element_type=F32, precision=precision)


def _dot_nt(a, b):
    return lax.dot_general(a, b, (((1,), (1,)), ((), ())), preferred_element_type=F32)


def _dot_tn(a, b):
    return lax.dot_general(a, b, (((0,), (0,)), ((), ())), preferred_element_type=F32)


def _silu(x):
    return x * jax.nn.sigmoid(x)


def _gelu(x):
    return 0.5 * x * (1.0 + lax.erf(x * (2.0 ** -0.5)))


def _softplus(x):
    return jnp.maximum(x, 0.0) + jnp.log1p(jnp.exp(-jnp.abs(x)))


def _params(*sem):
    return pltpu.CompilerParams(dimension_semantics=sem, vmem_limit_bytes=VMEM_LIMIT)


def _tril(n, strict=False):
    r = lax.broadcasted_iota(jnp.int32, (n, n), 0)
    c = lax.broadcasted_iota(jnp.int32, (n, n), 1)
    return (r > c) if strict else (r >= c)


def _norm_matmul_kernel(emit_xn, x_ref, nw_ref, w_ref, o_ref, *rest):
    xn_ref = rest[-1]

    @pl.when(pl.program_id(1) == 0)
    def _():
        x = x_ref[...]
        ms = jnp.mean(x * x, axis=-1, keepdims=True)
        xn = (x * lax.rsqrt(ms + RMS_EPS) * nw_ref[...]).astype(BF16)
        xn_ref[...] = xn
        if emit_xn:
            rest[0][...] = xn

    o_ref[...] = _dot(xn_ref[...], w_ref[...])


def norm_matmul(h, nw, w, *, tn, emit_xn=False, tm=512):
    t, d = h.shape
    n = w.shape[1]
    assert t % tm == 0 and n % tn == 0
    out_shape = [jax.ShapeDtypeStruct((t, n), F32)]
    out_specs = [pl.BlockSpec((tm, tn), lambda i, j: (i, j))]
    if emit_xn:
        out_shape.append(jax.ShapeDtypeStruct((t, d), BF16))
        out_specs.append(pl.BlockSpec((tm, d), lambda i, j: (i, 0)))
    res = pl.pallas_call(
        functools.partial(_norm_matmul_kernel, emit_xn),
        grid=(t // tm, n // tn),
        in_specs=[pl.BlockSpec((tm, d), lambda i, j: (i, 0)),
                  pl.BlockSpec((1, d), lambda i, j: (0, 0)),
                  pl.BlockSpec((d, tn), lambda i, j: (0, j))],
        out_specs=out_specs,
        out_shape=out_shape,
        scratch_shapes=[pltpu.VMEM((tm, d), BF16)],
        compiler_params=_params("parallel", "arbitrary"),
        name="norm_matmul",
    )(h, nw.reshape(1, d).astype(F32), w)
    return res if emit_xn else res[0]


def _hgrn_kernel(q_ref, f_ref, i_ref, g_ref, lb_ref, gn_ref, o_ref, st_ref, *, tb):
    c = HGRN_CHUNK

    @pl.when(pl.program_id(2) == 0)
    def _():
        st_ref[...] = jnp.zeros_like(st_ref)

    lb = lb_ref[...]
    q = _silu(q_ref[...]) * (HGRN_D ** -0.5)
    f = lb + (1.0 - lb) * jax.nn.sigmoid(f_ref[...])
    k = 1.0 - f
    logf = jnp.log(f)
    v = i_ref[...]
    tri = _tril(c).astype(F32)
    causal = _tril(c)
    st = st_ref[...]
    outs = []
    for n in range(tb // c):
        sl = slice(n * c, (n + 1) * c)
        cum = _dot(tri, logf[sl], HI)
        last = cum[c - 1:c]
        qe = (q[sl] * jnp.exp(cum)).astype(BF16)
        ke = (k[sl] * jnp.exp(-cum)).astype(BF16)
        kd = (k[sl] * jnp.exp(last - cum)).astype(BF16)
        vb = v[sl].astype(BF16)
        att = jnp.where(causal, _dot_nt(qe, ke), 0.0)
        outs.append(_dot(att.astype(BF16), vb) + _dot_nt(qe, st.astype(BF16)))
        st = jnp.exp(last) * st + _dot_tn(vb, kd)
    st_ref[...] = st
    o = jnp.concatenate(outs, axis=0)
    ms = jnp.mean(o * o, axis=-1, keepdims=True)
    o = o * lax.rsqrt(ms + RMS_EPS) * gn_ref[...]
    o_ref[...] = o * _silu(g_ref[...])


def hgrn2(proj, lb, gn_w, bsz, seq, *, tb=512):
    nh, d = HGRN_HEADS, HGRN_D
    ns = seq // tb
    row = lambda b, h, s: b * ns + s
    col = lambda off: (lambda b, h, s: (row(b, h, s), off + h))
    return pl.pallas_call(
        functools.partial(_hgrn_kernel, tb=tb),
        grid=(bsz, nh, ns),
        in_specs=[pl.BlockSpec((tb, d), col(0)),
                  pl.BlockSpec((tb, d), col(nh)),
                  pl.BlockSpec((tb, d), col(2 * nh)),
                  pl.BlockSpec((tb, d), col(3 * nh)),
                  pl.BlockSpec((None, 1, d), lambda b, h, s: (h, 0, 0)),
                  pl.BlockSpec((1, d), lambda b, h, s: (0, 0))],
        out_specs=pl.BlockSpec((tb, d), lambda b, h, s: (row(b, h, s), h)),
        out_shape=jax.ShapeDtypeStruct((bsz * seq, nh * d), F32),
        scratch_shapes=[pltpu.VMEM((d, d), F32)],
        compiler_params=_params("parallel", "parallel", "arbitrary"),
        name="hgrn2",
    )(proj, proj, proj, proj, lb.reshape(nh, 1, d).astype(F32), gn_w.reshape(1, d).astype(F32))


def _s5_weights(a_re, a_im, log_dt, b_re, b_im, c_re, c_im, d_skip):
    ln, gb = S5_L, S5_GB
    g, n = a_re.shape
    p = b_re.shape[-1]
    nb = g // gb
    a_re, a_im, b_re, b_im, c_re, c_im = (t.astype(F32) for t in (a_re, a_im, b_re, b_im, c_re, c_im))
    dt = jnp.exp(log_dt.astype(F32))[:, None]
    lam_re, lam_im = a_re * dt, a_im * dt
    def powers(exponents):
        kk = jnp.asarray(exponents, F32)[:, None, None]
        mag = jnp.exp(kk * lam_re)
        return mag * jnp.cos(kk * lam_im), mag * jnp.sin(kk * lam_im)

    pw_re, pw_im = powers(range(ln + 1))
    abar_re, abar_im = pw_re[1], pw_im[1]
    den = a_re * a_re + a_im * a_im
    coef_re = ((abar_re - 1.0) * a_re + abar_im * a_im) / den
    coef_im = (abar_im * a_re - (abar_re - 1.0) * a_im) / den
    bb_re = coef_re[..., None] * b_re - coef_im[..., None] * b_im
    bb_im = coef_re[..., None] * b_im + coef_im[..., None] * b_re
    eye = jnp.eye(gb, dtype=F32)

    rv, iv = powers(range(ln - 1, -1, -1))
    t1_re = rv[..., None] * bb_re - iv[..., None] * bb_im
    t1_im = rv[..., None] * bb_im + iv[..., None] * bb_re

    def expand_b(t1):
        w = jnp.einsum('tbgnp,gh->btgphn', t1.reshape(ln, nb, gb, n, p), eye)
        return w.reshape(nb, ln * gb * p, gb * n)

    w_b = jnp.concatenate([expand_b(t1_re), expand_b(t1_im)], axis=-1)

    qr, qi = pw_re[1:], pw_im[1:]
    t2_re = c_re[None] * qr[:, :, None, :] - c_im[None] * qi[:, :, None, :]
    t2_im = c_re[None] * qi[:, :, None, :] + c_im[None] * qr[:, :, None, :]

    def expand_c(t2):
        w = jnp.einsum('tbgpn,gh->bgnthp', t2.reshape(ln, nb, gb, p, n), eye)
        return w.reshape(nb, gb * n, ln * gb * p)

    w_c = jnp.concatenate([expand_c(t2_re), -expand_c(t2_im)], axis=1)

    cr = c_re[None] * pw_re[:ln, :, None, :] - c_im[None] * pw_im[:ln, :, None, :]
    ci = c_re[None] * pw_im[:ln, :, None, :] + c_im[None] * pw_re[:ln, :, None, :]
    kd = jnp.einsum('dgpn,gnq->dgpq', cr, bb_re) - jnp.einsum('dgpn,gnq->dgpq', ci, bb_im)
    tau = jnp.arange(ln)
    delta = tau[None, :] - tau[:, None]
    kt = jnp.where((delta >= 0)[:, :, None, None, None], kd[jnp.clip(delta, 0)], 0.0)
    w_k = jnp.einsum('stbgpq,gh->bsgqthp', kt.reshape(ln, ln, nb, gb, p, p), eye)
    w_k = w_k.reshape(nb, ln * gb * p, ln * gb * p)

    al = jnp.stack([pw_re[ln].reshape(nb, 1, gb * n), pw_im[ln].reshape(nb, 1, gb * n)], axis=1)
    dd = jnp.tile(d_skip.astype(F32).reshape(nb, 1, gb * p), (1, 1, ln))
    return w_b.astype(BF16), w_k.astype(BF16), w_c.astype(BF16), al.reshape(nb, 2, gb * n), dd


def _s5_kernel(*refs, cb):
    ln = S5_L
    u_refs = refs[:ln]
    wb_ref, wk_ref, wc_ref, al_ref, dd_ref, o_ref, s_ref, xp_ref, st_ref = refs[ln:]
    half = st_ref.shape[1] // 2

    @pl.when(pl.program_id(2) == 0)
    def _():
        st_ref[...] = jnp.zeros_like(st_ref)

    u = jnp.concatenate([r[...] for r in u_refs], axis=1)
    ub = u.astype(BF16)
    s_ref[...] = _dot(ub, wb_ref[...])
    ar = al_ref[0:1, :]
    ai = al_ref[1:2, :]

    def step(c, carry):
        xr, xi = carry
        xp_ref[pl.ds(c, 1), :] = jnp.concatenate([xr, xi], axis=1)
        srow = s_ref[pl.ds(c, 1), :]
        nr = ar * xr - ai * xi + srow[:, :half]
        ni = ar * xi + ai * xr + srow[:, half:]
        return nr, ni

    x0 = st_ref[...]
    xr, xi = lax.fori_loop(0, cb, step, (x0[:, :half], x0[:, half:]), unroll=8)
    st_ref[...] = jnp.concatenate([xr, xi], axis=1)
    y = _dot(ub, wk_ref[...]) + _dot(xp_ref[...].astype(BF16), wc_ref[...]) + dd_ref[...] * u
    o_ref[...] = _gelu(y)


def s5(proj, col0, weights, bsz, seq, *, cb=256):
    ln = S5_L
    w_b, w_k, w_c, al, dd = weights
    nb = w_b.shape[0]
    t, width = proj.shape
    nrow = t // ln
    ncb = seq // ln // cb
    assert width % LANES == 0 and col0 % LANES == 0 and ncb * cb * ln == seq
    wblk = width // LANES
    p2 = proj.reshape(nrow, ln * width)
    kw, ks = w_b.shape[1], w_b.shape[2]

    def u_spec(tau):
        return pl.BlockSpec((cb, LANES), lambda g, b, c: (b * ncb + c, tau * wblk + col0 // LANES + g))

    out = pl.pallas_call(
        functools.partial(_s5_kernel, cb=cb),
        grid=(nb, bsz, ncb),
        in_specs=[u_spec(tau) for tau in range(ln)] + [
            pl.BlockSpec((None, kw, ks), lambda g, b, c: (g, 0, 0)),
            pl.BlockSpec((None, kw, kw), lambda g, b, c: (g, 0, 0)),
            pl.BlockSpec((None, ks, kw), lambda g, b, c: (g, 0, 0)),
            pl.BlockSpec((None, 2, ks // 2), lambda g, b, c: (g, 0, 0)),
            pl.BlockSpec((None, 1, kw), lambda g, b, c: (g, 0, 0))],
        out_specs=pl.BlockSpec((None, cb, kw), lambda g, b, c: (g, b * ncb + c, 0)),
        out_shape=jax.ShapeDtypeStruct((nb, nrow, kw), F32),
        scratch_shapes=[pltpu.VMEM((cb, ks), F32), pltpu.VMEM((cb, ks), F32), pltpu.VMEM((1, ks), F32)],
        compiler_params=_params("parallel", "parallel", "arbitrary"),
        name="s5",
    )(*([p2] * ln), w_b, w_k, w_c, al, dd)
    return out.reshape(nb, t, LANES)


def _even_out_kernel(h_ref, oa_ref, y_ref, glu_ref, wa_ref, wb_ref, o_ref):
    y = jnp.concatenate([y_ref[i] for i in range(y_ref.shape[0])], axis=1)
    ob = y * jax.nn.sigmoid(_dot(y.astype(BF16), glu_ref[...]))
    o_ref[...] = h_ref[...] + _dot(oa_ref[...].astype(BF16), wa_ref[...]) + _dot(ob.astype(BF16), wb_ref[...])


def even_out(h, oa, y4, glu_w, w_a, w_b, *, tm=512):
    t, d = h.shape
    wa_rows, wb_rows = w_a.shape[0], w_b.shape[0]
    nb = y4.shape[0]
    return pl.pallas_call(
        _even_out_kernel,
        grid=(t // tm,),
        in_specs=[pl.BlockSpec((tm, d), lambda i: (i, 0)),
                  pl.BlockSpec((tm, wa_rows), lambda i: (i, 0)),
                  pl.BlockSpec((nb, tm, LANES), lambda i: (0, i, 0)),
                  pl.BlockSpec((wb_rows, wb_rows), lambda i: (0, 0)),
                  pl.BlockSpec((wa_rows, d), lambda i: (0, 0)),
                  pl.BlockSpec((wb_rows, d), lambda i: (0, 0))],
        out_specs=pl.BlockSpec((tm, d), lambda i: (i, 0)),
        out_shape=jax.ShapeDtypeStruct((t, d), F32),
        compiler_params=_params("parallel"),
        name="even_out",
    )(h, oa, y4, glu_w, w_a, w_b)


def _matmul_res_kernel(h_ref, a_ref, w_ref, o_ref):
    o_ref[...] = h_ref[...] + _dot(a_ref[...].astype(BF16), w_ref[...])


def matmul_residual(h, a, w, *, tm=512):
    t, d = h.shape
    k = a.shape[1]
    return pl.pallas_call(
        _matmul_res_kernel,
        grid=(t // tm,),
        in_specs=[pl.BlockSpec((tm, d), lambda i: (i, 0)),
                  pl.BlockSpec((tm, k), lambda i: (i, 0)),
                  pl.BlockSpec((k, d), lambda i: (0, 0))],
        out_specs=pl.BlockSpec((tm, d), lambda i: (i, 0)),
        out_shape=jax.ShapeDtypeStruct((t, d), F32),
        compiler_params=_params("parallel"),
        name="matmul_residual",
    )(h, a, w)


def _gdn_kernel(q_ref, k_ref, v_ref, z_ref, ba_ref, cwq_ref, cwk_ref, cwv_ref, selb_ref, sela_ref,
                nega_ref, dtb_ref, gn_ref, o_ref, st_ref, cb_ref, *, tb):
    c = GDN_CHUNK
    hist = 8

    @pl.when(pl.program_id(2) == 0)
    def _():
        st_ref[...] = jnp.zeros_like(st_ref)
        cb_ref[:, 0:hist, :] = jnp.zeros((3, hist, GDN_D), F32)

    cb_ref[0, hist:hist + tb, :] = q_ref[...]
    cb_ref[1, hist:hist + tb, :] = k_ref[...]
    cb_ref[2, hist:hist + tb, :] = v_ref[...]

    def conv(idx, w_ref):
        w = w_ref[...]
        base = hist - (GDN_CONV - 1)
        acc = w[0:1] * cb_ref[idx, base:base + tb, :]
        for j in range(1, GDN_CONV):
            acc = acc + w[j:j + 1] * cb_ref[idx, base + j:base + j + tb, :]
        return _silu(acc)

    qc, kc, vc = conv(0, cwq_ref), conv(1, cwk_ref), conv(2, cwv_ref)
    cb_ref[:, 0:hist, :] = cb_ref[:, tb:tb + hist, :]

    q = qc * lax.rsqrt(jnp.sum(qc * qc, axis=-1, keepdims=True) + L2_EPS) * (GDN_D ** -0.5)
    k = kc * lax.rsqrt(jnp.sum(kc * kc, axis=-1, keepdims=True) + L2_EPS)

    ba = ba_ref[...]
    beta = jax.nn.sigmoid(_dot(ba, selb_ref[...], HI))
    g = nega_ref[...] * _softplus(_dot(ba, sela_ref[...], HI) + dtb_ref[...])

    incl = _tril(c)
    strict = _tril(c, strict=True)
    tri = incl.astype(F32)
    eye = (lax.broadcasted_iota(jnp.int32, (c, c), 0) == lax.broadcasted_iota(jnp.int32, (c, c), 1)).astype(F32)
    st = st_ref[...]
    outs = []
    for n in range(tb // c):
        sl = slice(n * c, (n + 1) * c)
        gc = g[sl]
        dcol = _dot(tri, gc, HI)
        diff = _dot(tri, jnp.where(strict, gc[:, :c], 0.0), HI)
        lmask = jnp.where(incl, jnp.exp(jnp.where(incl, diff, 0.0)), 0.0)
        bc = beta[sl]
        kn, qn, vn = k[sl], q[sl], vc[sl]
        kb = kn * bc
        knb = kn.astype(BF16)
        a_mat = jnp.where(strict, _dot_nt(kb.astype(BF16), knb) * lmask, 0.0)
        x = -a_mat
        t_inv = eye + x
        for _ in range(int(math.log2(c)) - 1):
            x = _dot(x, x, HI)
            t_inv = t_inv + _dot(t_inv, x, HI)
        tib = t_inv.astype(BF16)
        edc = jnp.exp(dcol)
        value = _dot(tib, (vn * bc).astype(BF16))
        kcum = _dot(tib, (kb * edc).astype(BF16))
        attn = _dot_nt(qn.astype(BF16), knb) * lmask
        last = dcol[c - 1:c]
        ktail = kn * jnp.exp(last - dcol)
        sb = st.astype(BF16)
        vnew = value - _dot(kcum.astype(BF16), sb)
        vnb = vnew.astype(BF16)
        outs.append(_dot((qn * edc).astype(BF16), sb) + _dot(attn.astype(BF16), vnb))
        st = jnp.exp(last) * st + _dot_tn(ktail.astype(BF16), vnb)
    st_ref[...] = st
    o = jnp.concatenate(outs, axis=0)
    ms = jnp.mean(o * o, axis=-1, keepdims=True)
    o = o * lax.rsqrt(ms + RMS_EPS) * gn_ref[...]
    o_ref[...] = o * _silu(z_ref[...])


def gdn(proj, conv_w, a_log, dt_bias, gn_w, bsz, seq, *, tb=256):
    nh, d = GDN_HEADS, GDN_D
    ns = seq // tb
    row = lambda b, h, s: b * ns + s
    col = lambda off: (lambda b, h, s: (row(b, h, s), off + h))
    lane = jnp.arange(LANES)
    head = jnp.arange(nh)
    selb = jnp.broadcast_to((lane[None, :, None] == head[:, None, None]), (nh, LANES, LANES)).astype(F32)
    sela = jnp.broadcast_to((lane[None, :, None] == nh + head[:, None, None]), (nh, LANES, LANES)).astype(F32)
    nega = jnp.broadcast_to(-jnp.exp(a_log.astype(F32))[:, None, None], (nh, 1, LANES))
    dtb = jnp.broadcast_to(dt_bias.astype(F32)[:, None, None], (nh, 1, LANES))
    cw = conv_w.astype(F32)
    per_head = lambda b, h, s: (h, 0, 0)
    return pl.pallas_call(
        functools.partial(_gdn_kernel, tb=tb),
        grid=(bsz, nh, ns),
        in_specs=[pl.BlockSpec((tb, d), col(0)),
                  pl.BlockSpec((tb, d), col(nh)),
                  pl.BlockSpec((tb, d), col(2 * nh)),
                  pl.BlockSpec((tb, d), col(3 * nh)),
                  pl.BlockSpec((tb, LANES), lambda b, h, s: (row(b, h, s), 4 * nh)),
                  pl.BlockSpec((GDN_CONV, d), lambda b, h, s: (0, h)),
                  pl.BlockSpec((GDN_CONV, d), lambda b, h, s: (0, nh + h)),
                  pl.BlockSpec((GDN_CONV, d), lambda b, h, s: (0, 2 * nh + h)),
                  pl.BlockSpec((None, LANES, LANES), per_head),
                  pl.BlockSpec((None, LANES, LANES), per_head),
                  pl.BlockSpec((None, 1, LANES), per_head),
                  pl.BlockSpec((None, 1, LANES), per_head),
                  pl.BlockSpec((1, d), lambda b, h, s: (0, 0))],
        out_specs=pl.BlockSpec((tb, d), lambda b, h, s: (row(b, h, s), h)),
        out_shape=jax.ShapeDtypeStruct((bsz * seq, nh * d), F32),
        scratch_shapes=[pltpu.VMEM((d, d), F32), pltpu.VMEM((3, tb + 8, d), F32)],
        compiler_params=_params("parallel", "parallel", "arbitrary"),
        name="gdn",
    )(proj, proj, proj, proj, proj, cw, cw, cw, selb, sela, nega, dtb, gn_w.reshape(1, d).astype(F32))


def _top_values(work, count):
    vals = []
    for _ in range(count):
        m = jnp.max(work, axis=0, keepdims=True)
        vals.append(m)
        work = jnp.where(work == m, -MASKED, work)
    return vals


def _route_kernel(q_ref, keys_ref, thr_ref, s2_ref, p_ref, qq_ref):
    kk = PEER_TOPK
    for h in range(PEER_HEADS):
        sc, top = [], []
        for c in range(2):
            lo = (2 * h + c) * PEER_DHALF
            qh = q_ref[:, lo:lo + PEER_DHALF].astype(BF16)
            s = _dot_nt(keys_ref[c], qh)
            sc.append(s)
            top.append(_top_values(s, kk))
        keep1 = sc[0] >= top[0][kk - 1]
        keep2 = sc[1] >= top[1][kk - 1]
        v2 = jnp.concatenate(top[1], axis=0)
        cands = [top[0][a] + v2 for a in range(kk)]
        work = list(cands)
        best = []
        for _ in range(kk + 1):
            m = functools.reduce(jnp.maximum, work)
            m = jnp.max(m, axis=0, keepdims=True)
            best.append(m)
            work = [jnp.where(w == m, -MASKED, w) for w in work]
        theta = 0.5 * (best[kk - 1] + best[kk])
        cmax = best[0]
        z = functools.reduce(jnp.add, [jnp.where(cd >= theta, jnp.exp(cd - cmax), 0.0) for cd in cands])
        z = jnp.sum(z, axis=0, keepdims=True)
        thr_ref[h] = jnp.where(keep1, theta - sc[0], MASKED)
        s2_ref[h] = jnp.where(keep2, sc[1], -MASKED)
        p_ref[h] = jnp.where(keep1, jnp.exp(sc[0] - top[0][0]), 0.0) / z
        qq_ref[h] = jnp.where(keep2, jnp.exp(sc[1] - top[1][0]), 0.0)


def peer_route(q, keys, *, tbk=256):
    t = q.shape[0]
    shape = jax.ShapeDtypeStruct((PEER_HEADS, PEER_NKEYS, t), F32)
    spec = pl.BlockSpec((PEER_HEADS, PEER_NKEYS, tbk), lambda i: (0, 0, i))
    return pl.pallas_call(
        _route_kernel,
        grid=(t // tbk,),
        in_specs=[pl.BlockSpec((tbk, q.shape[1]), lambda i: (i, 0)),
                  pl.BlockSpec((2, PEER_NKEYS, PEER_DHALF), lambda i: (0, 0, 0))],
        out_specs=[spec] * 4,
        out_shape=[shape] * 4,
        compiler_params=_params("parallel"),
        name="peer_route",
    )(q, keys)


def _peer_ffn_kernel(x_ref, u_ref, vt_ref, thr_ref, s2_ref, p_ref, qq_ref, h_ref, nw_ref, o_ref,
                     ht_ref, wt_ref, acc_ref, *, final_norm, jt):
    eb, tb = ht_ref.shape
    nk = PEER_NKEYS
    j = pl.program_id(1)
    ht_ref[...] = _dot_nt(u_ref[...], x_ref[...])

    for il in range(eb // nk):
        for tl in range(tb // LANES):
            lanes = slice(tl * LANES, (tl + 1) * LANES)
            thr = [thr_ref[hd, il:il + 1, lanes] for hd in range(PEER_HEADS)]
            pr = [p_ref[hd, il:il + 1, lanes] for hd in range(PEER_HEADS)]
            for jj in range(nk // jt):
                rows = slice(il * nk + jj * jt, il * nk + (jj + 1) * jt)
                krows = slice(jj * jt, (jj + 1) * jt)
                gate = jnp.zeros((jt, LANES), F32)
                for hd in range(PEER_HEADS):
                    sel = s2_ref[hd, krows, lanes] >= thr[hd]
                    gate = gate + jnp.where(sel, pr[hd] * qq_ref[hd, krows, lanes], 0.0)
                wt_ref[rows, lanes] = (gate * _gelu(ht_ref[rows, lanes])).astype(BF16)
    part = _dot(vt_ref[...], wt_ref[...])

    @pl.when(j == 0)
    def _():
        acc_ref[...] = part

    @pl.when(j > 0)
    def _():
        acc_ref[...] += part

    @pl.when(j == pl.num_programs(1) - 1)
    def _():
        res = h_ref[...] + acc_ref[...].T
        if final_norm:
            ms = jnp.mean(res * res, axis=-1, keepdims=True)
            res = res * lax.rsqrt(ms + RMS_EPS) * nw_ref[...]
        o_ref[...] = res


def peer_ffn(xn, u, vt, route, h, norm_w, *, final_norm, tb=512, eb=1024, jt=32):
    t, d = h.shape
    ne = u.shape[0]
    thr, s2, pp, qq = route
    rspec = pl.BlockSpec((PEER_HEADS, PEER_NKEYS, tb), lambda i, j: (0, 0, i))
    ispec = pl.BlockSpec((PEER_HEADS, eb // PEER_NKEYS, tb), lambda i, j: (0, j, i))
    return pl.pallas_call(
        functools.partial(_peer_ffn_kernel, final_norm=final_norm, jt=jt),
        grid=(t // tb, ne // eb),
        in_specs=[pl.BlockSpec((tb, d), lambda i, j: (i, 0)),
                  pl.BlockSpec((eb, d), lambda i, j: (j, 0)),
                  pl.BlockSpec((d, eb), lambda i, j: (0, j)),
                  ispec, rspec, ispec, rspec,
                  pl.BlockSpec((tb, d), lambda i, j: (i, 0)),
                  pl.BlockSpec((1, d), lambda i, j: (0, 0))],
        out_specs=pl.BlockSpec((tb, d), lambda i, j: (i, 0)),
        out_shape=jax.ShapeDtypeStruct((t, d), F32),
        scratch_shapes=[pltpu.VMEM((eb, tb), F32), pltpu.VMEM((eb, tb), BF16), pltpu.VMEM((d, tb), F32)],
        compiler_params=_params("parallel", "arbitrary"),
        name="peer_ffn",
    )(xn, u, vt, thr, s2, pp, qq, h, norm_w.reshape(1, d).astype(F32))


def peer_layer(h, norm_w, w_q, sub_keys, u_tab, v_tab, out_norm_w, *, final_norm):
    q, xn = norm_matmul(h, norm_w, w_q.astype(BF16), tn=1024, emit_xn=True)
    route = peer_route(q, sub_keys.astype(BF16))
    return peer_ffn(xn, u_tab.astype(BF16), v_tab.astype(BF16).T, route, h, out_norm_w, final_norm=final_norm)


def kernel(x, norm_mix_w, norm_ffn_w, norm_out_w, ev_in_w, ev_out_w, hgrn_lb, hgrn_gn_w, s5_a_re, s5_a_im, s5_log_dt, s5_b_re, s5_b_im, s5_c_re, s5_c_im, s5_d, s5_glu_w, od_in_w, od_out_w, gdn_conv_w, gdn_a_log, gdn_dt_bias, gdn_gn_w, peer_wq, peer_sub_keys, peer_u, peer_v):
    bsz, seq, d = x.shape
    depth = norm_mix_w.shape[0]
    t = bsz * seq
    h = x.reshape(t, d).astype(F32)
    lb_all = jnp.cumsum(jax.nn.softmax(hgrn_lb.astype(F32), axis=0), axis=0)
    hgrn_w = HGRN_HEADS * HGRN_D
    for layer in range(depth):
        j = layer // 2
        if layer % 2 == 0:
            proj = norm_matmul(h, norm_mix_w[layer], ev_in_w[j].astype(BF16), tn=512)
            o_a = hgrn2(proj, lb_all[j], hgrn_gn_w[j], bsz, seq)
            weights = _s5_weights(s5_a_re[j], s5_a_im[j], s5_log_dt[j], s5_b_re[j], s5_b_im[j],
                                  s5_c_re[j], s5_c_im[j], s5_d[j])
            y4 = s5(proj, 4 * hgrn_w, weights, bsz, seq)
            w_out = ev_out_w[j].astype(BF16)
            h = even_out(h, o_a, y4, s5_glu_w[j].astype(BF16), w_out[:hgrn_w], w_out[hgrn_w:])
        else:
            w_in = od_in_w[j]
            pad = (-w_in.shape[1]) % (3 * LANES)
            w_in = jnp.pad(w_in, ((0, 0), (0, pad))).astype(BF16)
            proj = norm_matmul(h, norm_mix_w[layer], w_in, tn=w_in.shape[1] // 3)
            o_c = gdn(proj, gdn_conv_w[j], gdn_a_log[j], gdn_dt_bias[j], gdn_gn_w[j], bsz, seq)
            h = matmul_residual(h, o_c, od_out_w[j].astype(BF16))
        h = peer_layer(h, norm_ffn_w[layer], peer_wq[layer], peer_sub_keys[layer], peer_u[layer],
                       peer_v[layer], norm_out_w, final_norm=(layer == depth - 1))
    return h.reshape(bsz, seq, d)
```

```python
import functools
import math

import jax
import jax.numpy as jnp
from jax import lax
from jax.experimental import pallas as pl
from jax.experimental.pallas import tpu as pltpu

F32 = jnp.float32
BF16 = jnp.bfloat16
HI = lax.Precision.HIGHEST

RMS_EPS = 1e-6
L2_EPS = 1e-6
LANES = 128
VMEM_LIMIT = 56 * 1024 * 1024

HGRN_HEADS, HGRN_D, HGRN_CHUNK = 4, 128, 32
S5_GROUP, S5_STATE, S5_L = 16, 64, 8
S5_GB = LANES // S5_GROUP
GDN_HEADS, GDN_D, GDN_CONV, GDN_CHUNK = 8, 128, 4, 64
PEER_HEADS, PEER_NKEYS, PEER_DHALF, PEER_TOPK = 8, 128, 128, 16
MASKED = 1e30


def _dot(a, b, precision=None):
    return jnp.dot(a, b, preferred_element_type=F32, precision=precision)


def _dot_nt(a, b):
    return lax.dot_general(a, b, (((1,), (1,)), ((), ())), preferred_element_type=F32)


def _dot_tn(a, b):
    return lax.dot_general(a, b, (((0,), (0,)), ((), ())), preferred_element_type=F32)


def _silu(x):
    return x * jax.nn.sigmoid(x)


def _gelu(x):
    return 0.5 * x * (1.0 + lax.erf(x * (2.0 ** -0.5)))


def _softplus(x):
    return jnp.maximum(x, 0.0) + jnp.log1p(jnp.exp(-jnp.abs(x)))


def _params(*sem):
    return pltpu.CompilerParams(dimension_semantics=sem, vmem_limit_bytes=VMEM_LIMIT)


def _tril(n, strict=False):
    r = lax.broadcasted_iota(jnp.int32, (n, n), 0)
    c = lax.broadcasted_iota(jnp.int32, (n, n), 1)
    return (r > c) if strict else (r >= c)


def _norm_matmul_kernel(emit_xn, x_ref, nw_ref, w_ref, o_ref, *rest):
    xn_ref = rest[-1]

    @pl.when(pl.program_id(1) == 0)
    def _():
        x = x_ref[...]
        ms = jnp.mean(x * x, axis=-1, keepdims=True)
        xn = (x * lax.rsqrt(ms + RMS_EPS) * nw_ref[...]).astype(BF16)
        xn_ref[...] = xn
        if emit_xn:
            rest[0][...] = xn

    o_ref[...] = _dot(xn_ref[...], w_ref[...])


def norm_matmul(h, nw, w, *, tn, emit_xn=False, tm=512):
    t, d = h.shape
    n = w.shape[1]
    assert t % tm == 0 and n % tn == 0
    out_shape = [jax.ShapeDtypeStruct((t, n), F32)]
    out_specs = [pl.BlockSpec((tm, tn), lambda i, j: (i, j))]
    if emit_xn:
        out_shape.append(jax.ShapeDtypeStruct((t, d), BF16))
        out_specs.append(pl.BlockSpec((tm, d), lambda i, j: (i, 0)))
    res = pl.pallas_call(
        functools.partial(_norm_matmul_kernel, emit_xn),
        grid=(t // tm, n // tn),
        in_specs=[pl.BlockSpec((tm, d), lambda i, j: (i, 0)),
                  pl.BlockSpec((1, d), lambda i, j: (0, 0)),
                  pl.BlockSpec((d, tn), lambda i, j: (0, j))],
        out_specs=out_specs,
        out_shape=out_shape,
        scratch_shapes=[pltpu.VMEM((tm, d), BF16)],
        compiler_params=_params("parallel", "arbitrary"),
        name="norm_matmul",
    )(h, nw.reshape(1, d).astype(F32), w)
    return res if emit_xn else res[0]


def _hgrn_kernel(q_ref, f_ref, i_ref, g_ref, lb_ref, gn_ref, o_ref, st_ref, *, tb):
    c = HGRN_CHUNK

    @pl.when(pl.program_id(2) == 0)
    def _():
        st_ref[...] = jnp.zeros_like(st_ref)

    lb = lb_ref[...]
    q = _silu(q_ref[...]) * (HGRN_D ** -0.5)
    f = lb + (1.0 - lb) * jax.nn.sigmoid(f_ref[...])
    k = 1.0 - f
    logf = jnp.log(f)
    v = i_ref[...]
    tri = _tril(c).astype(F32)
    causal = _tril(c)
    st = st_ref[...]
    outs = []
    for n in range(tb // c):
        sl = slice(n * c, (n + 1) * c)
        cum = _dot(tri, logf[sl], HI)
        last = cum[c - 1:c]
        qe = (q[sl] * jnp.exp(cum)).astype(BF16)
        ke = (k[sl] * jnp.exp(-cum)).astype(BF16)
        kd = (k[sl] * jnp.exp(last - cum)).astype(BF16)
        vb = v[sl].astype(BF16)
        att = jnp.where(causal, _dot_nt(qe, ke), 0.0)
        outs.append(_dot(att.astype(BF16), vb) + _dot_nt(qe, st.astype(BF16)))
        st = jnp.exp(last) * st + _dot_tn(vb, kd)
    st_ref[...] = st
    o = jnp.concatenate(outs, axis=0)
    ms = jnp.mean(o * o, axis=-1, keepdims=True)
    o = o * lax.rsqrt(ms + RMS_EPS) * gn_ref[...]
    o_ref[...] = o * _silu(g_ref[...])


def hgrn2(proj, lb, gn_w, bsz, seq, *, tb=512):
    nh, d = HGRN_HEADS, HGRN_D
    ns = seq // tb
    row = lambda b, h, s: b * ns + s
    col = lambda off: (lambda b, h, s: (row(b, h, s), off + h))
    return pl.pallas_call(
        functools.partial(_hgrn_kernel, tb=tb),
        grid=(bsz, nh, ns),
        in_specs=[pl.BlockSpec((tb, d), col(0)),
                  pl.BlockSpec((tb, d), col(nh)),
                  pl.BlockSpec((tb, d), col(2 * nh)),
                  pl.BlockSpec((tb, d), col(3 * nh)),
                  pl.BlockSpec((None, 1, d), lambda b, h, s: (h, 0, 0)),
                  pl.BlockSpec((1, d), lambda b, h, s: (0, 0))],
        out_specs=pl.BlockSpec((tb, d), lambda b, h, s: (row(b, h, s), h)),
        out_shape=jax.ShapeDtypeStruct((bsz * seq, nh * d), F32),
        scratch_shapes=[pltpu.VMEM((d, d), F32)],
        compiler_params=_params("parallel", "parallel", "arbitrary"),
        name="hgrn2",
    )(proj, proj, proj, proj, lb.reshape(nh, 1, d).astype(F32), gn_w.reshape(1, d).astype(F32))


def _s5_weights(a_re, a_im, log_dt, b_re, b_im, c_re, c_im, d_skip):
    ln, gb = S5_L, S5_GB
    g, n = a_re.shape
    p = b_re.shape[-1]
    nb = g // gb
    a_re, a_im, b_re, b_im, c_re, c_im = (t.astype(F32) for t in (a_re, a_im, b_re, b_im, c_re, c_im))
    dt = jnp.exp(log_dt.astype(F32))[:, None]
    lam_re, lam_im = a_re * dt, a_im * dt
    def powers(exponents):
        kk = jnp.asarray(exponents, F32)[:, None, None]
        mag = jnp.exp(kk * lam_re)
        return mag * jnp.cos(kk * lam_im), mag * jnp.sin(kk * lam_im)

    pw_re, pw_im = powers(range(ln + 1))
    abar_re, abar_im = pw_re[1], pw_im[1]
    den = a_re * a_re + a_im * a_im
    coef_re = ((abar_re - 1.0) * a_re + abar_im * a_im) / den
    coef_im = (abar_im * a_re - (abar_re - 1.0) * a_im) / den
    bb_re = coef_re[..., None] * b_re - coef_im[..., None] * b_im
    bb_im = coef_re[..., None] * b_im + coef_im[..., None] * b_re
    eye = jnp.eye(gb, dtype=F32)

    rv, iv = powers(range(ln - 1, -1, -1))
    t1_re = rv[..., None] * bb_re - iv[..., None] * bb_im
    t1_im = rv[..., None] * bb_im + iv[..., None] * bb_re

    def expand_b(t1):
        w = jnp.einsum('tbgnp,gh->btgphn', t1.reshape(ln, nb, gb, n, p), eye)
        return w.reshape(nb, ln * gb * p, gb * n)

    w_b = jnp.concatenate([expand_b(t1_re), expand_b(t1_im)], axis=-1)

    qr, qi = pw_re[1:], pw_im[1:]
    t2_re = c_re[None] * qr[:, :, None, :] - c_im[None] * qi[:, :, None, :]
    t2_im = c_re[None] * qi[:, :, None, :] + c_im[None] * qr[:, :, None, :]

    def expand_c(t2):
        w = jnp.einsum('tbgpn,gh->bgnthp', t2.reshape(ln, nb, gb, p, n), eye)
        return w.reshape(nb, gb * n, ln * gb * p)

    w_c = jnp.concatenate([expand_c(t2_re), -expand_c(t2_im)], axis=1)

    cr = c_re[None] * pw_re[:ln, :, None, :] - c_im[None] * pw_im[:ln, :, None, :]
    ci = c_re[None] * pw_im[:ln, :, None, :] + c_im[None] * pw_re[:ln, :, None, :]
    kd = jnp.einsum('dgpn,gnq->dgpq', cr, bb_re) - jnp.einsum('dgpn,gnq->dgpq', ci, bb_im)
    tau = jnp.arange(ln)
    delta = tau[None, :] - tau[:, None]
    kt = jnp.where((delta >= 0)[:, :, None, None, None], kd[jnp.clip(delta, 0)], 0.0)
    w_k = jnp.einsum('stbgpq,gh->bsgqthp', kt.reshape(ln, ln, nb, gb, p, p), eye)
    w_k = w_k.reshape(nb, ln * gb * p, ln * gb * p)

    al = jnp.stack([pw_re[ln].reshape(nb, 1, gb * n), pw_im[ln].reshape(nb, 1, gb * n)], axis=1)
    dd = jnp.tile(d_skip.astype(F32).reshape(nb, 1, gb * p), (1, 1, ln))
    return w_b.astype(BF16), w_k.astype(BF16), w_c.astype(BF16), al.reshape(nb, 2, gb * n), dd


def _s5_kernel(*refs, cb):
    ln = S5_L
    u_refs = refs[:ln]
    wb_ref, wk_ref, wc_ref, al_ref, dd_ref, o_ref, s_ref, xp_ref, st_ref = refs[ln:]
    half = st_ref.shape[1] // 2

    @pl.when(pl.program_id(2) == 0)
    def _():
        st_ref[...] = jnp.zeros_like(st_ref)

    u = jnp.concatenate([r[...] for r in u_refs], axis=1)
    ub = u.astype(BF16)
    s_ref[...] = _dot(ub, wb_ref[...])
    ar = al_ref[0:1, :]
    ai = al_ref[1:2, :]

    def step(c, carry):
        xr, xi = carry
        xp_ref[pl.ds(c, 1), :] = jnp.concatenate([xr, xi], axis=1)
        srow = s_ref[pl.ds(c, 1), :]
        nr = ar * xr - ai * xi + srow[:, :half]
        ni = ar * xi + ai * xr + srow[:, half:]
        return nr, ni

    x0 = st_ref[...]
    xr, xi = lax.fori_loop(0, cb, step, (x0[:, :half], x0[:, half:]), unroll=8)
    st_ref[...] = jnp.concatenate([xr, xi], axis=1)
    y = _dot(ub, wk_ref[...]) + _dot(xp_ref[...].astype(BF16), wc_ref[...]) + dd_ref[...] * u
    o_ref[...] = _gelu(y)


def s5(proj, col0, weights, bsz, seq, *, cb=256):
    ln = S5_L
    w_b, w_k, w_c, al, dd = weights
    nb = w_b.shape[0]
    t, width = proj.shape
    nrow = t // ln
    ncb = seq // ln // cb
    assert width % LANES == 0 and col0 % LANES == 0 and ncb * cb * ln == seq
    wblk = width // LANES
    p2 = proj.reshape(nrow, ln * width)
    kw, ks = w_b.shape[1], w_b.shape[2]

    def u_spec(tau):
        return pl.BlockSpec((cb, LANES), lambda g, b, c: (b * ncb + c, tau * wblk + col0 // LANES + g))

    out = pl.pallas_call(
        functools.partial(_s5_kernel, cb=cb),
        grid=(nb, bsz, ncb),
        in_specs=[u_spec(tau) for tau in range(ln)] + [
            pl.BlockSpec((None, kw, ks), lambda g, b, c: (g, 0, 0)),
            pl.BlockSpec((None, kw, kw), lambda g, b, c: (g, 0, 0)),
            pl.BlockSpec((None, ks, kw), lambda g, b, c: (g, 0, 0)),
            pl.BlockSpec((None, 2, ks // 2), lambda g, b, c: (g, 0, 0)),
            pl.BlockSpec((None, 1, kw), lambda g, b, c: (g, 0, 0))],
        out_specs=pl.BlockSpec((None, cb, kw), lambda g, b, c: (g, b * ncb + c, 0)),
        out_shape=jax.ShapeDtypeStruct((nb, nrow, kw), F32),
        scratch_shapes=[pltpu.VMEM((cb, ks), F32), pltpu.VMEM((cb, ks), F32), pltpu.VMEM((1, ks), F32)],
        compiler_params=_params("parallel", "parallel", "arbitrary"),
        name="s5",
    )(*([p2] * ln), w_b, w_k, w_c, al, dd)
    return out.reshape(nb, t, LANES)


def _even_out_kernel(h_ref, oa_ref, y_ref, glu_ref, wa_ref, wb_ref, o_ref):
    y = jnp.concatenate([y_ref[i] for i in range(y_ref.shape[0])], axis=1)
    ob = y * jax.nn.sigmoid(_dot(y.astype(BF16), glu_ref[...]))
    o_ref[...] = h_ref[...] + _dot(oa_ref[...].astype(BF16), wa_ref[...]) + _dot(ob.astype(BF16), wb_ref[...])


def even_out(h, oa, y4, glu_w, w_a, w_b, *, tm=512):
    t, d = h.shape
    wa_rows, wb_rows = w_a.shape[0], w_b.shape[0]
    nb = y4.shape[0]
    return pl.pallas_call(
        _even_out_kernel,
        grid=(t // tm,),
        in_specs=[pl.BlockSpec((tm, d), lambda i: (i, 0)),
                  pl.BlockSpec((tm, wa_rows), lambda i: (i, 0)),
                  pl.BlockSpec((nb, tm, LANES), lambda i: (0, i, 0)),
                  pl.BlockSpec((wb_rows, wb_rows), lambda i: (0, 0)),
                  pl.BlockSpec((wa_rows, d), lambda i: (0, 0)),
                  pl.BlockSpec((wb_rows, d), lambda i: (0, 0))],
        out_specs=pl.BlockSpec((tm, d), lambda i: (i, 0)),
        out_shape=jax.ShapeDtypeStruct((t, d), F32),
        compiler_params=_params("parallel"),
        name="even_out",
    )(h, oa, y4, glu_w, w_a, w_b)


def _matmul_res_kernel(h_ref, a_ref, w_ref, o_ref):
    o_ref[...] = h_ref[...] + _dot(a_ref[...].astype(BF16), w_ref[...])


def matmul_residual(h, a, w, *, tm=512):
    t, d = h.shape
    k = a.shape[1]
    return pl.pallas_call(
        _matmul_res_kernel,
        grid=(t // tm,),
        in_specs=[pl.BlockSpec((tm, d), lambda i: (i, 0)),
                  pl.BlockSpec((tm, k), lambda i: (i, 0)),
                  pl.BlockSpec((k, d), lambda i: (0, 0))],
        out_specs=pl.BlockSpec((tm, d), lambda i: (i, 0)),
        out_shape=jax.ShapeDtypeStruct((t, d), F32),
        compiler_params=_params("parallel"),
        name="matmul_residual",
    )(h, a, w)


def _split_bf16(x):
    hi = x.astype(BF16)
    return hi, (x - hi.astype(F32)).astype(BF16)


def _dot_split(xs, ys):
    (xh, xl), (yh, yl) = xs, ys
    return _dot(xh, yh) + (_dot(xh, yl) + _dot(xl, yh))


def _gdn_kernel(q_ref, k_ref, v_ref, z_ref, ba_ref, cw_ref, nega_ref, dtb_ref, gn_ref, o_ref,
                st_ref, cb_ref, *, tb):
    c, d, nh = GDN_CHUNK, GDN_D, GDN_HEADS
    width = nh * d
    hist = 8

    @pl.when(pl.program_id(1) == 0)
    def _():
        st_ref[...] = jnp.zeros_like(st_ref)
        cb_ref[:, 0:hist, :] = jnp.zeros((3, hist, width), F32)

    cb_ref[0, hist:hist + tb, :] = q_ref[...]
    cb_ref[1, hist:hist + tb, :] = k_ref[...]
    cb_ref[2, hist:hist + tb, :] = v_ref[...]

    def conv(idx, lanes):
        base = hist - (GDN_CONV - 1)
        col = slice(idx * width + lanes.start, idx * width + lanes.stop)
        acc = cw_ref[0:1, col] * cb_ref[idx, base:base + tb, lanes]
        for j in range(1, GDN_CONV):
            acc = acc + cw_ref[j:j + 1, col] * cb_ref[idx, base + j:base + j + tb, lanes]
        return _silu(acc)

    ba = ba_ref[...]
    beta_all = jax.nn.sigmoid(ba)
    g_all = nega_ref[...] * _softplus(ba + dtb_ref[...])

    incl = _tril(c)
    strict = _tril(c, strict=True)
    tri = incl.astype(F32)
    eye = (lax.broadcasted_iota(jnp.int32, (c, c), 0) == lax.broadcasted_iota(jnp.int32, (c, c), 1)).astype(F32)
    decay = [_dot(tri, g_all[n * c:(n + 1) * c], HI) for n in range(tb // c)]
    decay_t = [dc.T for dc in decay]

    nc = tb // c
    units = [(h, n) for n in range(nc) for h in range(nh)]
    prep = {}
    for h in range(nh):
        lanes = slice(h * d, (h + 1) * d)
        qc, kc, vc = conv(0, lanes), conv(1, lanes), conv(2, lanes)
        q = qc * lax.rsqrt(jnp.sum(qc * qc, axis=-1, keepdims=True) + L2_EPS) * (d ** -0.5)
        k = kc * lax.rsqrt(jnp.sum(kc * kc, axis=-1, keepdims=True) + L2_EPS)
        for n in range(nc):
            sl = slice(n * c, (n + 1) * c)
            dcol = jnp.broadcast_to(decay[n][:, nh + h:nh + h + 1], (c, d))
            diff = dcol[:, :c] - decay_t[n][nh + h:nh + h + 1, :]
            lmask = jnp.where(incl, jnp.exp(jnp.where(incl, diff, 0.0)), 0.0)
            bc = jnp.broadcast_to(beta_all[sl, h:h + 1], (c, d))
            prep[h, n] = dict(dcol=dcol, lmask=lmask, bc=bc, k=k[sl], q=q[sl], v=vc[sl])
    cb_ref[:, 0:hist, :] = cb_ref[:, tb:tb + hist, :]

    xs, t_inv = {}, {}
    for u in units:
        p = prep[u]
        p['kb'] = p['k'] * p['bc']
        p['knb'] = p['k'].astype(BF16)
        a_mat = jnp.where(strict, _dot_nt(p['kb'].astype(BF16), p['knb']) * p['lmask'], 0.0)
        t_inv[u] = eye - a_mat
        xs[u] = _split_bf16(-a_mat)
    for _ in range(int(math.log2(c)) - 1):
        for u in units:
            xs[u] = _split_bf16(_dot_split(xs[u], xs[u]))
        for u in units:
            t_inv[u] = t_inv[u] + _dot_split(_split_bf16(t_inv[u]), xs[u])
    for u in units:
        p = prep[u]
        tib = t_inv[u].astype(BF16)
        edc = jnp.exp(p['dcol'])
        last = p['dcol'][c - 1:c]
        p['value'] = _dot(tib, (p['v'] * p['bc']).astype(BF16))
        p['kcum'] = _dot(tib, (p['kb'] * edc).astype(BF16)).astype(BF16)
        p['attn'] = (_dot_nt(p['q'].astype(BF16), p['knb']) * p['lmask']).astype(BF16)
        p['qdec'] = (p['q'] * edc).astype(BF16)
        p['ktail'] = (p['k'] * jnp.exp(last - p['dcol'])).astype(BF16)
        p['tot'] = jnp.exp(last)

    st = [st_ref[h] for h in range(nh)]
    outs = {}
    for n in range(nc):
        sb = [s.astype(BF16) for s in st]
        vnb = [(prep[h, n]['value'] - _dot(prep[h, n]['kcum'], sb[h])).astype(BF16) for h in range(nh)]
        for h in range(nh):
            p = prep[h, n]
            outs[h, n] = _dot(p['qdec'], sb[h]) + _dot(p['attn'], vnb[h])
        st = [prep[h, n]['tot'] * st[h] + _dot_tn(prep[h, n]['ktail'], vnb[h]) for h in range(nh)]
    for h in range(nh):
        lanes = slice(h * d, (h + 1) * d)
        st_ref[h] = st[h]
        o = jnp.concatenate([outs[h, n] for n in range(nc)], axis=0)
        ms = jnp.mean(o * o, axis=-1, keepdims=True)
        o = o * lax.rsqrt(ms + RMS_EPS) * gn_ref[...]
        o_ref[:, lanes] = o * _silu(z_ref[:, lanes])


def gdn(proj, conv_w, a_log, dt_bias, gn_w, bsz, seq, *, tb=128):
    nh, d = GDN_HEADS, GDN_D
    width = nh * d
    ns = seq // tb
    row = lambda b, s: b * ns + s
    lane = jnp.arange(LANES)
    in_a = (lane >= nh) & (lane < 2 * nh)
    idx = jnp.clip(lane - nh, 0, nh - 1)
    nega = jnp.where(in_a, -jnp.exp(a_log.astype(F32))[idx], 0.0).reshape(1, LANES)
    dtb = jnp.where(in_a, dt_bias.astype(F32)[idx], 0.0).reshape(1, LANES)
    const = lambda b, s: (0, 0)
    return pl.pallas_call(
        functools.partial(_gdn_kernel, tb=tb),
        grid=(bsz, ns),
        in_specs=[pl.BlockSpec((tb, width), lambda b, s: (row(b, s), 0)),
                  pl.BlockSpec((tb, width), lambda b, s: (row(b, s), 1)),
                  pl.BlockSpec((tb, width), lambda b, s: (row(b, s), 2)),
                  pl.BlockSpec((tb, width), lambda b, s: (row(b, s), 3)),
                  pl.BlockSpec((tb, LANES), lambda b, s: (row(b, s), 4 * nh)),
                  pl.BlockSpec((GDN_CONV, 3 * width), const),
                  pl.BlockSpec((1, LANES), const),
                  pl.BlockSpec((1, LANES), const),
                  pl.BlockSpec((1, d), const)],
        out_specs=pl.BlockSpec((tb, width), lambda b, s: (row(b, s), 0)),
        out_shape=jax.ShapeDtypeStruct((bsz * seq, width), F32),
        scratch_shapes=[pltpu.VMEM((nh, d, d), F32), pltpu.VMEM((3, tb + 8, width), F32)],
        compiler_params=_params("parallel", "arbitrary"),
        name="gdn",
    )(proj, proj, proj, proj, proj, conv_w.astype(F32), nega, dtb, gn_w.reshape(1, d).astype(F32))


def _top_values(work, count):
    vals = []
    for _ in range(count):
        m = jnp.max(work, axis=0, keepdims=True)
        vals.append(m)
        work = jnp.where(work == m, -MASKED, work)
    return vals


def _route_kernel(q_ref, keys_ref, thr_ref, s2_ref, p_ref, qq_ref):
    kk = PEER_TOPK
    sub = 8
    for h in range(PEER_HEADS):
        sc, top = [], []
        for c in range(2):
            lo = (2 * h + c) * PEER_DHALF
            qh = q_ref[:, lo:lo + PEER_DHALF].astype(BF16)
            s = _dot_nt(keys_ref[c], qh)
            sc.append(s)
            top.append(_top_values(s, kk))
        keep1 = sc[0] >= top[0][kk - 1]
        keep2 = sc[1] >= top[1][kk - 1]
        v1 = jnp.concatenate(top[0], axis=0)
        v2 = jnp.concatenate(top[1], axis=0)
        cands = [top[0][0] + v2[:sub], top[0][0] + v2[sub:]]
        cands += [top[0][a] + v2[:sub] for a in range(1, sub)]
        cands += [v1[sub:] + top[1][0]]
        work = list(cands)
        best = []
        for _ in range(kk + 1):
            m = jnp.max(functools.reduce(jnp.maximum, work), axis=0, keepdims=True)
            best.append(m)
            work = [jnp.where(w == m, -MASKED, w) for w in work]
        theta = 0.5 * (best[kk - 1] + best[kk])
        cmax = best[0]
        z = functools.reduce(jnp.add, [jnp.sum(jnp.where(cd >= theta, jnp.exp(cd - cmax), 0.0), axis=0,
                                               keepdims=True) for cd in cands])
        thr_ref[h] = jnp.where(keep1, theta - sc[0], MASKED)
        s2_ref[h] = jnp.where(keep2, sc[1], -MASKED)
        p_ref[h] = jnp.where(keep1, jnp.exp(sc[0] - top[0][0]), 0.0) / z
        qq_ref[h] = jnp.where(keep2, jnp.exp(sc[1] - top[1][0]), 0.0)


def peer_route(q, keys, *, tbk=256):
    t = q.shape[0]
    spec = pl.BlockSpec((PEER_HEADS, PEER_NKEYS, tbk), lambda i: (0, 0, i))
    shape = lambda dt: jax.ShapeDtypeStruct((PEER_HEADS, PEER_NKEYS, t), dt)
    return pl.pallas_call(
        _route_kernel,
        grid=(t // tbk,),
        in_specs=[pl.BlockSpec((tbk, q.shape[1]), lambda i: (i, 0)),
                  pl.BlockSpec((2, PEER_NKEYS, PEER_DHALF), lambda i: (0, 0, 0))],
        out_specs=[spec] * 4,
        out_shape=[shape(F32)] * 4,
        compiler_params=_params("parallel"),
        name="peer_route",
    )(q, keys)


def _peer_ffn_kernel(x_ref, u_ref, vt_ref, thr_ref, s2_ref, p_ref, qq_ref, h_ref, nw_ref, o_ref,
                     ht_ref, wt_ref, acc_ref, *, final_norm):
    eb, tb = ht_ref.shape
    nk = PEER_NKEYS
    jt = 8
    j = pl.program_id(1)
    ht_ref[...] = _dot_nt(u_ref[...], x_ref[...])

    for il in range(eb // nk):
        for tl in range(tb // LANES):
            lanes = slice(tl * LANES, (tl + 1) * LANES)
            thr = [jnp.broadcast_to(thr_ref[hd, il:il + 1, lanes], (jt, LANES)) for hd in range(PEER_HEADS)]
            pr = [jnp.broadcast_to(p_ref[hd, il:il + 1, lanes], (jt, LANES)) for hd in range(PEER_HEADS)]
            for jj in range(nk // jt):
                rows = slice(il * nk + jj * jt, il * nk + (jj + 1) * jt)
                krows = slice(jj * jt, (jj + 1) * jt)
                gate = jnp.zeros((jt, LANES), F32)
                for hd in range(PEER_HEADS):
                    sel = s2_ref[hd, krows, lanes] >= thr[hd]
                    gate = gate + jnp.where(sel, pr[hd] * qq_ref[hd, krows, lanes], 0.0)
                wt_ref[rows, lanes] = (gate * _gelu(ht_ref[rows, lanes])).astype(BF16)
    part = _dot(vt_ref[...], wt_ref[...])

    @pl.when(j == 0)
    def _():
        acc_ref[...] = part

    @pl.when(j > 0)
    def _():
        acc_ref[...] += part

    @pl.when(j == pl.num_programs(1) - 1)
    def _():
        res = h_ref[...] + acc_ref[...].T
        if final_norm:
            ms = jnp.mean(res * res, axis=-1, keepdims=True)
            res = res * lax.rsqrt(ms + RMS_EPS) * nw_ref[...]
        o_ref[...] = res


def peer_ffn(xn, u, vt, route, h, norm_w, *, final_norm, tb=512, eb=1024):
    t, d = h.shape
    ne = u.shape[0]
    thr, s2, pp, qq = route
    rspec = pl.BlockSpec((PEER_HEADS, PEER_NKEYS, tb), lambda i, j: (0, 0, i))
    ispec = pl.BlockSpec((PEER_HEADS, eb // PEER_NKEYS, tb), lambda i, j: (0, j, i))
    return pl.pallas_call(
        functools.partial(_peer_ffn_kernel, final_norm=final_norm),
        grid=(t // tb, ne // eb),
        in_specs=[pl.BlockSpec((tb, d), lambda i, j: (i, 0)),
                  pl.BlockSpec((eb, d), lambda i, j: (j, 0)),
                  pl.BlockSpec((d, eb), lambda i, j: (0, j)),
                  ispec, rspec, ispec, rspec,
                  pl.BlockSpec((tb, d), lambda i, j: (i, 0)),
                  pl.BlockSpec((1, d), lambda i, j: (0, 0))],
        out_specs=pl.BlockSpec((tb, d), lambda i, j: (i, 0)),
        out_shape=jax.ShapeDtypeStruct((t, d), F32),
        scratch_shapes=[pltpu.VMEM((eb, tb), F32), pltpu.VMEM((eb, tb), BF16), pltpu.VMEM((d, tb), F32)],
        compiler_params=_params("parallel", "arbitrary"),
        name="peer_ffn",
    )(xn, u, vt, thr, s2, pp, qq, h, norm_w.reshape(1, d).astype(F32))


def peer_layer(h, norm_w, w_q, sub_keys, u_tab, v_tab, out_norm_w, *, final_norm):
    q, xn = norm_matmul(h, norm_w, w_q.astype(BF16), tn=1024, emit_xn=True)
    route = peer_route(q, sub_keys.astype(BF16))
    return peer_ffn(xn, u_tab.astype(BF16), v_tab.astype(BF16).T, route, h, out_norm_w, final_norm=final_norm)


def kernel(x, norm_mix_w, norm_ffn_w, norm_out_w, ev_in_w, ev_out_w, hgrn_lb, hgrn_gn_w, s5_a_re, s5_a_im, s5_log_dt, s5_b_re, s5_b_im, s5_c_re, s5_c_im, s5_d, s5_glu_w, od_in_w, od_out_w, gdn_conv_w, gdn_a_log, gdn_dt_bias, gdn_gn_w, peer_wq, peer_sub_keys, peer_u, peer_v):
    bsz, seq, d = x.shape
    depth = norm_mix_w.shape[0]
    t = bsz * seq
    h = x.reshape(t, d).astype(F32)
    lb_all = jnp.cumsum(jax.nn.softmax(hgrn_lb.astype(F32), axis=0), axis=0)
    hgrn_w = HGRN_HEADS * HGRN_D
    for layer in range(depth):
        j = layer // 2
        if layer % 2 == 0:
            proj = norm_matmul(h, norm_mix_w[layer], ev_in_w[j].astype(BF16), tn=512)
            o_a = hgrn2(proj, lb_all[j], hgrn_gn_w[j], bsz, seq)
            weights = _s5_weights(s5_a_re[j], s5_a_im[j], s5_log_dt[j], s5_b_re[j], s5_b_im[j],
                                  s5_c_re[j], s5_c_im[j], s5_d[j])
            y4 = s5(proj, 4 * hgrn_w, weights, bsz, seq)
            w_out = ev_out_w[j].astype(BF16)
            h = even_out(h, o_a, y4, s5_glu_w[j].astype(BF16), w_out[:hgrn_w], w_out[hgrn_w:])
        else:
            w_in = od_in_w[j]
            pad = (-w_in.shape[1]) % (3 * LANES)
            w_in = jnp.pad(w_in, ((0, 0), (0, pad))).astype(BF16)
            proj = norm_matmul(h, norm_mix_w[layer], w_in, tn=w_in.shape[1] // 3)
            o_c = gdn(proj, gdn_conv_w[j], gdn_a_log[j], gdn_dt_bias[j], gdn_gn_w[j], bsz, seq)
            h = matmul_residual(h, o_c, od_out_w[j].astype(BF16))
        h = peer_layer(h, norm_ffn_w[layer], peer_wq[layer], peer_sub_keys[layer], peer_u[layer],
                       peer_v[layer], norm_out_w, final_norm=(layer == depth - 1))
    return h.reshape(bsz, seq, d)
```

```python
import functools
import math

import jax
import jax.numpy as jnp
from jax import lax
from jax.experimental import pallas as pl
from jax.experimental.pallas import tpu as pltpu

F32 = jnp.float32
BF16 = jnp.bfloat16
HI = lax.Precision.HIGHEST

RMS_EPS = 1e-6
L2_EPS = 1e-6
LANES = 128
VMEM_LIMIT = 56 * 1024 * 1024

HGRN_HEADS, HGRN_D, HGRN_CHUNK = 4, 128, 32
S5_GROUP, S5_STATE, S5_L = 16, 64, 8
S5_GB = LANES // S5_GROUP
GDN_HEADS, GDN_D, GDN_CONV, GDN_CHUNK = 8, 128, 4, 64
PEER_HEADS, PEER_NKEYS, PEER_DHALF, PEER_TOPK = 8, 128, 128, 16
MASKED = 1e30


def _dot(a, b, precision=None):
    return jnp.dot(a, b, preferred_element_type=F32, precision=precision)


def _dot_nt(a, b):
    return lax.dot_general(a, b, (((1,), (1,)), ((), ())), preferred_element_type=F32)


def _dot_tn(a, b):
    return lax.dot_general(a, b, (((0,), (0,)), ((), ())), preferred_element_type=F32)


def _silu(x):
    return x * jax.nn.sigmoid(x)


def _gelu(x):
    return 0.5 * x * (1.0 + lax.erf(x * (2.0 ** -0.5)))


def _softplus(x):
    return jnp.maximum(x, 0.0) + jnp.log1p(jnp.exp(-jnp.abs(x)))


def _params(*sem):
    return pltpu.CompilerParams(dimension_semantics=sem, vmem_limit_bytes=VMEM_LIMIT)


def _tril(n, strict=False):
    r = lax.broadcasted_iota(jnp.int32, (n, n), 0)
    c = lax.broadcasted_iota(jnp.int32, (n, n), 1)
    return (r > c) if strict else (r >= c)


def _norm_matmul_kernel(emit_xn, x_ref, nw_ref, w_ref, o_ref, *rest):
    xn_ref = rest[-1]

    @pl.when(pl.program_id(1) == 0)
    def _():
        x = x_ref[...]
        ms = jnp.mean(x * x, axis=-1, keepdims=True)
        xn = (x * lax.rsqrt(ms + RMS_EPS) * nw_ref[...]).astype(BF16)
        xn_ref[...] = xn
        if emit_xn:
            rest[0][...] = xn

    o_ref[...] = _dot(xn_ref[...], w_ref[...])


def norm_matmul(h, nw, w, *, tn, emit_xn=False, tm=512):
    t, d = h.shape
    n = w.shape[1]
    assert t % tm == 0 and n % tn == 0
    out_shape = [jax.ShapeDtypeStruct((t, n), F32)]
    out_specs = [pl.BlockSpec((tm, tn), lambda i, j: (i, j))]
    if emit_xn:
        out_shape.append(jax.ShapeDtypeStruct((t, d), BF16))
        out_specs.append(pl.BlockSpec((tm, d), lambda i, j: (i, 0)))
    res = pl.pallas_call(
        functools.partial(_norm_matmul_kernel, emit_xn),
        grid=(t // tm, n // tn),
        in_specs=[pl.BlockSpec((tm, d), lambda i, j: (i, 0)),
                  pl.BlockSpec((1, d), lambda i, j: (0, 0)),
                  pl.BlockSpec((d, tn), lambda i, j: (0, j))],
        out_specs=out_specs,
        out_shape=out_shape,
        scratch_shapes=[pltpu.VMEM((tm, d), BF16)],
        compiler_params=_params("parallel", "arbitrary"),
        name="norm_matmul",
    )(h, nw.reshape(1, d).astype(F32), w)
    return res if emit_xn else res[0]


def _hgrn_kernel(q_ref, f_ref, i_ref, g_ref, lb_ref, gn_ref, o_ref, st_ref, *, tb):
    c = HGRN_CHUNK

    @pl.when(pl.program_id(2) == 0)
    def _():
        st_ref[...] = jnp.zeros_like(st_ref)

    lb = lb_ref[...]
    q = _silu(q_ref[...]) * (HGRN_D ** -0.5)
    f = lb + (1.0 - lb) * jax.nn.sigmoid(f_ref[...])
    k = 1.0 - f
    logf = jnp.log(f)
    v = i_ref[...]
    tri = _tril(c).astype(F32)
    causal = _tril(c)
    st = st_ref[...]
    outs = []
    for n in range(tb // c):
        sl = slice(n * c, (n + 1) * c)
        cum = _dot(tri, logf[sl], HI)
        last = cum[c - 1:c]
        qe = (q[sl] * jnp.exp(cum)).astype(BF16)
        ke = (k[sl] * jnp.exp(-cum)).astype(BF16)
        kd = (k[sl] * jnp.exp(last - cum)).astype(BF16)
        vb = v[sl].astype(BF16)
        att = jnp.where(causal, _dot_nt(qe, ke), 0.0)
        outs.append(_dot(att.astype(BF16), vb) + _dot_nt(qe, st.astype(BF16)))
        st = jnp.exp(last) * st + _dot_tn(vb, kd)
    st_ref[...] = st
    o = jnp.concatenate(outs, axis=0)
    ms = jnp.mean(o * o, axis=-1, keepdims=True)
    o = o * lax.rsqrt(ms + RMS_EPS) * gn_ref[...]
    o_ref[...] = o * _silu(g_ref[...])


def hgrn2(proj, lb, gn_w, bsz, seq, *, tb=512):
    nh, d = HGRN_HEADS, HGRN_D
    ns = seq // tb
    row = lambda b, h, s: b * ns + s
    col = lambda off: (lambda b, h, s: (row(b, h, s), off + h))
    return pl.pallas_call(
        functools.partial(_hgrn_kernel, tb=tb),
        grid=(bsz, nh, ns),
        in_specs=[pl.BlockSpec((tb, d), col(0)),
                  pl.BlockSpec((tb, d), col(nh)),
                  pl.BlockSpec((tb, d), col(2 * nh)),
                  pl.BlockSpec((tb, d), col(3 * nh)),
                  pl.BlockSpec((None, 1, d), lambda b, h, s: (h, 0, 0)),
                  pl.BlockSpec((1, d), lambda b, h, s: (0, 0))],
        out_specs=pl.BlockSpec((tb, d), lambda b, h, s: (row(b, h, s), h)),
        out_shape=jax.ShapeDtypeStruct((bsz * seq, nh * d), F32),
        scratch_shapes=[pltpu.VMEM((d, d), F32)],
        compiler_params=_params("parallel", "parallel", "arbitrary"),
        name="hgrn2",
    )(proj, proj, proj, proj, lb.reshape(nh, 1, d).astype(F32), gn_w.reshape(1, d).astype(F32))


def _s5_weights(a_re, a_im, log_dt, b_re, b_im, c_re, c_im, d_skip):
    ln, gb = S5_L, S5_GB
    g, n = a_re.shape
    p = b_re.shape[-1]
    nb = g // gb
    a_re, a_im, b_re, b_im, c_re, c_im = (t.astype(F32) for t in (a_re, a_im, b_re, b_im, c_re, c_im))
    dt = jnp.exp(log_dt.astype(F32))[:, None]
    lam_re, lam_im = a_re * dt, a_im * dt
    def powers(exponents):
        kk = jnp.asarray(exponents, F32)[:, None, None]
        mag = jnp.exp(kk * lam_re)
        return mag * jnp.cos(kk * lam_im), mag * jnp.sin(kk * lam_im)

    pw_re, pw_im = powers(range(ln + 1))
    abar_re, abar_im = pw_re[1], pw_im[1]
    den = a_re * a_re + a_im * a_im
    coef_re = ((abar_re - 1.0) * a_re + abar_im * a_im) / den
    coef_im = (abar_im * a_re - (abar_re - 1.0) * a_im) / den
    bb_re = coef_re[..., None] * b_re - coef_im[..., None] * b_im
    bb_im = coef_re[..., None] * b_im + coef_im[..., None] * b_re
    eye = jnp.eye(gb, dtype=F32)

    rv, iv = powers(range(ln - 1, -1, -1))
    t1_re = rv[..., None] * bb_re - iv[..., None] * bb_im
    t1_im = rv[..., None] * bb_im + iv[..., None] * bb_re

    def expand_b(t1):
        w = jnp.einsum('tbgnp,gh->btgphn', t1.reshape(ln, nb, gb, n, p), eye)
        return w.reshape(nb, ln * gb * p, gb * n)

    w_b = jnp.concatenate([expand_b(t1_re), expand_b(t1_im)], axis=-1)

    qr, qi = pw_re[1:], pw_im[1:]
    t2_re = c_re[None] * qr[:, :, None, :] - c_im[None] * qi[:, :, None, :]
    t2_im = c_re[None] * qi[:, :, None, :] + c_im[None] * qr[:, :, None, :]

    def expand_c(t2):
        w = jnp.einsum('tbgpn,gh->bgnthp', t2.reshape(ln, nb, gb, p, n), eye)
        return w.reshape(nb, gb * n, ln * gb * p)

    w_c = jnp.concatenate([expand_c(t2_re), -expand_c(t2_im)], axis=1)

    cr = c_re[None] * pw_re[:ln, :, None, :] - c_im[None] * pw_im[:ln, :, None, :]
    ci = c_re[None] * pw_im[:ln, :, None, :] + c_im[None] * pw_re[:ln, :, None, :]
    kd = jnp.einsum('dgpn,gnq->dgpq', cr, bb_re) - jnp.einsum('dgpn,gnq->dgpq', ci, bb_im)
    tau = jnp.arange(ln)
    delta = tau[None, :] - tau[:, None]
    kt = jnp.where((delta >= 0)[:, :, None, None, None], kd[jnp.clip(delta, 0)], 0.0)
    w_k = jnp.einsum('stbgpq,gh->bsgqthp', kt.reshape(ln, ln, nb, gb, p, p), eye)
    w_k = w_k.reshape(nb, ln * gb * p, ln * gb * p)

    al = jnp.stack([pw_re[ln].reshape(nb, 1, gb * n), pw_im[ln].reshape(nb, 1, gb * n)], axis=1)
    dd = jnp.tile(d_skip.astype(F32).reshape(nb, 1, gb * p), (1, 1, ln))
    return w_b.astype(BF16), w_k.astype(BF16), w_c.astype(BF16), al.reshape(nb, 2, gb * n), dd


def _s5_kernel(*refs, cb):
    ln = S5_L
    u_refs = refs[:ln]
    wb_ref, wk_ref, wc_ref, al_ref, dd_ref, o_ref, s_ref, xp_ref, st_ref = refs[ln:]
    half = st_ref.shape[1] // 2

    @pl.when(pl.program_id(2) == 0)
    def _():
        st_ref[...] = jnp.zeros_like(st_ref)

    u = jnp.concatenate([r[...] for r in u_refs], axis=1)
    ub = u.astype(BF16)
    s_ref[...] = _dot(ub, wb_ref[...])
    ar = al_ref[0:1, :]
    ai = al_ref[1:2, :]

    def step(c, carry):
        xr, xi = carry
        xp_ref[pl.ds(c, 1), :] = jnp.concatenate([xr, xi], axis=1)
        srow = s_ref[pl.ds(c, 1), :]
        nr = ar * xr - ai * xi + srow[:, :half]
        ni = ar * xi + ai * xr + srow[:, half:]
        return nr, ni

    x0 = st_ref[...]
    xr, xi = lax.fori_loop(0, cb, step, (x0[:, :half], x0[:, half:]), unroll=8)
    st_ref[...] = jnp.concatenate([xr, xi], axis=1)
    y = _dot(ub, wk_ref[...]) + _dot(xp_ref[...].astype(BF16), wc_ref[...]) + dd_ref[...] * u
    o_ref[...] = _gelu(y)


def s5(proj, col0, weights, bsz, seq, *, cb=256):
    ln = S5_L
    w_b, w_k, w_c, al, dd = weights
    nb = w_b.shape[0]
    t, width = proj.shape
    nrow = t // ln
    ncb = seq // ln // cb
    assert width % LANES == 0 and col0 % LANES == 0 and ncb * cb * ln == seq
    wblk = width // LANES
    p2 = proj.reshape(nrow, ln * width)
    kw, ks = w_b.shape[1], w_b.shape[2]

    def u_spec(tau):
        return pl.BlockSpec((cb, LANES), lambda g, b, c: (b * ncb + c, tau * wblk + col0 // LANES + g))

    out = pl.pallas_call(
        functools.partial(_s5_kernel, cb=cb),
        grid=(nb, bsz, ncb),
        in_specs=[u_spec(tau) for tau in range(ln)] + [
            pl.BlockSpec((None, kw, ks), lambda g, b, c: (g, 0, 0)),
            pl.BlockSpec((None, kw, kw), lambda g, b, c: (g, 0, 0)),
            pl.BlockSpec((None, ks, kw), lambda g, b, c: (g, 0, 0)),
            pl.BlockSpec((None, 2, ks // 2), lambda g, b, c: (g, 0, 0)),
            pl.BlockSpec((None, 1, kw), lambda g, b, c: (g, 0, 0))],
        out_specs=pl.BlockSpec((None, cb, kw), lambda g, b, c: (g, b * ncb + c, 0)),
        out_shape=jax.ShapeDtypeStruct((nb, nrow, kw), F32),
        scratch_shapes=[pltpu.VMEM((cb, ks), F32), pltpu.VMEM((cb, ks), F32), pltpu.VMEM((1, ks), F32)],
        compiler_params=_params("parallel", "parallel", "arbitrary"),
        name="s5",
    )(*([p2] * ln), w_b, w_k, w_c, al, dd)
    return out.reshape(nb, t, LANES)


def _even_out_kernel(h_ref, oa_ref, y_ref, glu_ref, wa_ref, wb_ref, o_ref):
    y = jnp.concatenate([y_ref[i] for i in range(y_ref.shape[0])], axis=1)
    ob = y * jax.nn.sigmoid(_dot(y.astype(BF16), glu_ref[...]))
    o_ref[...] = h_ref[...] + _dot(oa_ref[...].astype(BF16), wa_ref[...]) + _dot(ob.astype(BF16), wb_ref[...])


def even_out(h, oa, y4, glu_w, w_a, w_b, *, tm=512):
    t, d = h.shape
    wa_rows, wb_rows = w_a.shape[0], w_b.shape[0]
    nb = y4.shape[0]
    return pl.pallas_call(
        _even_out_kernel,
        grid=(t // tm,),
        in_specs=[pl.BlockSpec((tm, d), lambda i: (i, 0)),
                  pl.BlockSpec((tm, wa_rows), lambda i: (i, 0)),
                  pl.BlockSpec((nb, tm, LANES), lambda i: (0, i, 0)),
                  pl.BlockSpec((wb_rows, wb_rows), lambda i: (0, 0)),
                  pl.BlockSpec((wa_rows, d), lambda i: (0, 0)),
                  pl.BlockSpec((wb_rows, d), lambda i: (0, 0))],
        out_specs=pl.BlockSpec((tm, d), lambda i: (i, 0)),
        out_shape=jax.ShapeDtypeStruct((t, d), F32),
        compiler_params=_params("parallel"),
        name="even_out",
    )(h, oa, y4, glu_w, w_a, w_b)


def _matmul_res_kernel(h_ref, a_ref, w_ref, o_ref):
    o_ref[...] = h_ref[...] + _dot(a_ref[...].astype(BF16), w_ref[...])


def matmul_residual(h, a, w, *, tm=512):
    t, d = h.shape
    k = a.shape[1]
    return pl.pallas_call(
        _matmul_res_kernel,
        grid=(t // tm,),
        in_specs=[pl.BlockSpec((tm, d), lambda i: (i, 0)),
                  pl.BlockSpec((tm, k), lambda i: (i, 0)),
                  pl.BlockSpec((k, d), lambda i: (0, 0))],
        out_specs=pl.BlockSpec((tm, d), lambda i: (i, 0)),
        out_shape=jax.ShapeDtypeStruct((t, d), F32),
        compiler_params=_params("parallel"),
        name="matmul_residual",
    )(h, a, w)


def _split_bf16(x):
    hi = x.astype(BF16)
    return hi, (x - hi.astype(F32)).astype(BF16)


def _dot_split(xs, ys):
    (xh, xl), (yh, yl) = xs, ys
    return _dot(xh, yh) + (_dot(xh, yl) + _dot(xl, yh))


def _gdn_kernel(q_ref, k_ref, v_ref, z_ref, ba_ref, cw_ref, nega_ref, dtb_ref, gn_ref, o_ref,
                st_ref, cb_ref, *, tb):
    c, d, nh = GDN_CHUNK, GDN_D, GDN_HEADS
    width = nh * d
    hist = 8

    @pl.when(pl.program_id(1) == 0)
    def _():
        st_ref[...] = jnp.zeros_like(st_ref)
        cb_ref[:, 0:hist, :] = jnp.zeros((3, hist, width), F32)

    cb_ref[0, hist:hist + tb, :] = q_ref[...]
    cb_ref[1, hist:hist + tb, :] = k_ref[...]
    cb_ref[2, hist:hist + tb, :] = v_ref[...]

    def conv(idx, lanes):
        base = hist - (GDN_CONV - 1)
        col = slice(idx * width + lanes.start, idx * width + lanes.stop)
        acc = cw_ref[0:1, col] * cb_ref[idx, base:base + tb, lanes]
        for j in range(1, GDN_CONV):
            acc = acc + cw_ref[j:j + 1, col] * cb_ref[idx, base + j:base + j + tb, lanes]
        return _silu(acc)

    ba = ba_ref[...]
    beta_all = jax.nn.sigmoid(ba)
    g_all = nega_ref[...] * _softplus(ba + dtb_ref[...])

    incl = _tril(c)
    strict = _tril(c, strict=True)
    tri = incl.astype(F32)
    eye = (lax.broadcasted_iota(jnp.int32, (c, c), 0) == lax.broadcasted_iota(jnp.int32, (c, c), 1)).astype(F32)
    decay = [_dot(tri, g_all[n * c:(n + 1) * c], HI) for n in range(tb // c)]
    decay_t = [dc.T for dc in decay]

    nc = tb // c
    units = [(h, n) for n in range(nc) for h in range(nh)]
    prep = {}
    for h in range(nh):
        lanes = slice(h * d, (h + 1) * d)
        qc, kc, vc = conv(0, lanes), conv(1, lanes), conv(2, lanes)
        q = qc * lax.rsqrt(jnp.sum(qc * qc, axis=-1, keepdims=True) + L2_EPS) * (d ** -0.5)
        k = kc * lax.rsqrt(jnp.sum(kc * kc, axis=-1, keepdims=True) + L2_EPS)
        for n in range(nc):
            sl = slice(n * c, (n + 1) * c)
            dcol = jnp.broadcast_to(decay[n][:, nh + h:nh + h + 1], (c, d))
            diff = dcol[:, :c] - decay_t[n][nh + h:nh + h + 1, :]
            lmask = jnp.where(incl, jnp.exp(jnp.where(incl, diff, 0.0)), 0.0)
            bc = jnp.broadcast_to(beta_all[sl, h:h + 1], (c, d))
            prep[h, n] = dict(dcol=dcol, lmask=lmask, bc=bc, k=k[sl], q=q[sl], v=vc[sl])
    cb_ref[:, 0:hist, :] = cb_ref[:, tb:tb + hist, :]

    xs, t_inv = {}, {}
    for u in units:
        p = prep[u]
        p['kb'] = p['k'] * p['bc']
        p['knb'] = p['k'].astype(BF16)
        a_mat = jnp.where(strict, _dot_nt(p['kb'].astype(BF16), p['knb']) * p['lmask'], 0.0)
        t_inv[u] = eye - a_mat
        xs[u] = _split_bf16(-a_mat)
    for _ in range(int(math.log2(c)) - 1):
        for u in units:
            xs[u] = _split_bf16(_dot_split(xs[u], xs[u]))
        for u in units:
            t_inv[u] = t_inv[u] + _dot_split(_split_bf16(t_inv[u]), xs[u])
    for u in units:
        p = prep[u]
        tib = t_inv[u].astype(BF16)
        edc = jnp.exp(p['dcol'])
        last = p['dcol'][c - 1:c]
        p['value'] = _dot(tib, (p['v'] * p['bc']).astype(BF16))
        p['kcum'] = _dot(tib, (p['kb'] * edc).astype(BF16)).astype(BF16)
        p['attn'] = (_dot_nt(p['q'].astype(BF16), p['knb']) * p['lmask']).astype(BF16)
        p['qdec'] = (p['q'] * edc).astype(BF16)
        p['ktail'] = (p['k'] * jnp.exp(last - p['dcol'])).astype(BF16)
        p['tot'] = jnp.exp(last)

    st = [st_ref[h] for h in range(nh)]
    outs = {}
    for n in range(nc):
        sb = [s.astype(BF16) for s in st]
        vnb = [(prep[h, n]['value'] - _dot(prep[h, n]['kcum'], sb[h])).astype(BF16) for h in range(nh)]
        for h in range(nh):
            p = prep[h, n]
            outs[h, n] = _dot(p['qdec'], sb[h]) + _dot(p['attn'], vnb[h])
        st = [prep[h, n]['tot'] * st[h] + _dot_tn(prep[h, n]['ktail'], vnb[h]) for h in range(nh)]
    for h in range(nh):
        lanes = slice(h * d, (h + 1) * d)
        st_ref[h] = st[h]
        o = jnp.concatenate([outs[h, n] for n in range(nc)], axis=0)
        ms = jnp.mean(o * o, axis=-1, keepdims=True)
        o = o * lax.rsqrt(ms + RMS_EPS) * gn_ref[...]
        o_ref[:, lanes] = o * _silu(z_ref[:, lanes])


def gdn(proj, conv_w, a_log, dt_bias, gn_w, bsz, seq, *, tb=128):
    nh, d = GDN_HEADS, GDN_D
    width = nh * d
    ns = seq // tb
    row = lambda b, s: b * ns + s
    lane = jnp.arange(LANES)
    in_a = (lane >= nh) & (lane < 2 * nh)
    idx = jnp.clip(lane - nh, 0, nh - 1)
    nega = jnp.where(in_a, -jnp.exp(a_log.astype(F32))[idx], 0.0).reshape(1, LANES)
    dtb = jnp.where(in_a, dt_bias.astype(F32)[idx], 0.0).reshape(1, LANES)
    const = lambda b, s: (0, 0)
    return pl.pallas_call(
        functools.partial(_gdn_kernel, tb=tb),
        grid=(bsz, ns),
        in_specs=[pl.BlockSpec((tb, width), lambda b, s: (row(b, s), 0)),
                  pl.BlockSpec((tb, width), lambda b, s: (row(b, s), 1)),
                  pl.BlockSpec((tb, width), lambda b, s: (row(b, s), 2)),
                  pl.BlockSpec((tb, width), lambda b, s: (row(b, s), 3)),
                  pl.BlockSpec((tb, LANES), lambda b, s: (row(b, s), 4 * nh)),
                  pl.BlockSpec((GDN_CONV, 3 * width), const),
                  pl.BlockSpec((1, LANES), const),
                  pl.BlockSpec((1, LANES), const),
                  pl.BlockSpec((1, d), const)],
        out_specs=pl.BlockSpec((tb, width), lambda b, s: (row(b, s), 0)),
        out_shape=jax.ShapeDtypeStruct((bsz * seq, width), F32),
        scratch_shapes=[pltpu.VMEM((nh, d, d), F32), pltpu.VMEM((3, tb + 8, width), F32)],
        compiler_params=_params("parallel", "arbitrary"),
        name="gdn",
    )(proj, proj, proj, proj, proj, conv_w.astype(F32), nega, dtb, gn_w.reshape(1, d).astype(F32))


NO_RANK = 100.0
SUBLANES = 8


def _pack_pair(lo, hi):
    lo_bits = pltpu.bitcast(lo.astype(BF16).astype(F32), jnp.uint32)
    hi_bits = pltpu.bitcast(hi.astype(BF16).astype(F32), jnp.uint32)
    return hi_bits | (lo_bits >> 16)


def _pack_rows(x):
    sub = SUBLANES
    return jnp.concatenate([_pack_pair(x[2 * m * sub:(2 * m + 1) * sub], x[(2 * m + 1) * sub:(2 * m + 2) * sub])
                            for m in range(x.shape[0] // (2 * sub))], axis=0)


def _ranked_top(s, count, with_rank):
    work, vals = s, []
    rank = jnp.full(s.shape, NO_RANK, F32) if with_rank else None
    for r in range(count):
        m = jnp.max(work, axis=0, keepdims=True)
        hit = work == m
        vals.append(m)
        if with_rank:
            rank = jnp.where(hit, float(r + 1), rank)
        work = jnp.where(hit, -MASKED, work)
    return vals, rank


def _route_kernel(q_ref, keys_ref, cnt_ref, rank_ref, p_ref, qq_ref):
    kk = PEER_TOPK
    sub = SUBLANES
    for h in range(PEER_HEADS):
        sc, top, rank = [], [], []
        for c in range(2):
            lo = (2 * h + c) * PEER_DHALF
            qh = q_ref[:, lo:lo + PEER_DHALF].astype(BF16)
            s = _dot_nt(keys_ref[c], qh)
            vals, rk = _ranked_top(s, kk, with_rank=(c == 1))
            sc.append(s)
            top.append(vals)
            rank.append(rk)
        v1 = jnp.concatenate(top[0], axis=0)
        v2 = jnp.concatenate(top[1], axis=0)
        cands = [top[0][0] + v2[:sub], top[0][0] + v2[sub:]]
        cands += [top[0][a] + v2[:sub] for a in range(1, sub)]
        cands += [v1[sub:] + top[1][0]]
        work = list(cands)
        best = []
        for _ in range(kk + 1):
            m = jnp.max(functools.reduce(jnp.maximum, work), axis=0, keepdims=True)
            best.append(m)
            work = [jnp.where(w == m, -MASKED, w) for w in work]
        theta = 0.5 * (best[kk - 1] + best[kk])
        cmax = best[0]
        taken = [cd >= theta for cd in cands]
        z = functools.reduce(jnp.add, [jnp.sum(jnp.where(tk, jnp.exp(cd - cmax), 0.0), axis=0, keepdims=True)
                                       for tk, cd in zip(taken, cands)])
        ones = [jnp.where(tk, 1.0, 0.0) for tk in taken]
        per_row = [jnp.sum(o, axis=0, keepdims=True) for o in ones[:sub + 1]]
        cnt_rows = [per_row[0] + per_row[1]] + per_row[2:] + [ones[sub + 1][r:r + 1] for r in range(kk - sub)]
        cnt = jnp.zeros_like(sc[0])
        for a in range(kk):
            cnt = jnp.where(sc[0] == top[0][a], cnt_rows[a], cnt)
        pp = jnp.where(sc[0] >= top[0][kk - 1], jnp.exp(sc[0] - top[0][0]), 0.0) / z
        qq = jnp.where(rank[1] < NO_RANK, jnp.exp(sc[1] - top[1][0]), 0.0)
        cnt_ref[h] = _pack_pair(cnt, cnt)
        p_ref[h] = _pack_pair(pp, pp)
        rank_ref[h] = _pack_rows(rank[1])
        qq_ref[h] = _pack_rows(qq)


def peer_route(q, keys, *, tbk=256):
    t = q.shape[0]
    nk = PEER_NKEYS
    first = pl.BlockSpec((PEER_HEADS, nk, tbk), lambda i: (0, 0, i))
    second = pl.BlockSpec((PEER_HEADS, nk // 2, tbk), lambda i: (0, 0, i))
    shape = lambda rows: jax.ShapeDtypeStruct((PEER_HEADS, rows, t), jnp.uint32)
    return pl.pallas_call(
        _route_kernel,
        grid=(t // tbk,),
        in_specs=[pl.BlockSpec((tbk, q.shape[1]), lambda i: (i, 0)),
                  pl.BlockSpec((2, nk, PEER_DHALF), lambda i: (0, 0, 0))],
        out_specs=[first, second, first, second],
        out_shape=[shape(nk), shape(nk // 2), shape(nk), shape(nk // 2)],
        compiler_params=_params("parallel"),
        name="peer_route",
    )(q, keys)


def _peer_ffn_kernel(x_ref, u_ref, vt_ref, cnt_ref, rank_ref, p_ref, qq_ref, h_ref, nw_ref, o_ref,
                     ht_ref, wt_ref, acc_ref, *, final_norm, tsub):
    eb, tb = ht_ref.shape
    nk = PEER_NKEYS
    sub = SUBLANES
    j = pl.program_id(1)

    @pl.when(j == 0)
    def _():
        acc_ref[...] = jnp.zeros_like(acc_ref)

    def pairs(words):
        return pltpu.bitcast(words, BF16)

    def activations(tok):
        ht_ref[:, tok] = _dot_nt(u_ref[...], x_ref[tok, :])

    def weigh(tok):
        for tl in range(tok.start // LANES, tok.stop // LANES):
            lanes = slice(tl * LANES, (tl + 1) * LANES)
            for il in range(eb // nk):
                cnt = [pairs(jnp.broadcast_to(cnt_ref[hd, il:il + 1, lanes], (sub, LANES)))
                       for hd in range(PEER_HEADS)]
                pr = [pairs(jnp.broadcast_to(p_ref[hd, il:il + 1, lanes], (sub, LANES)))
                      for hd in range(PEER_HEADS)]
                for m in range(nk // (2 * sub)):
                    krows = slice(m * sub, (m + 1) * sub)
                    terms = []
                    for hd in range(PEER_HEADS):
                        sel = pairs(rank_ref[hd, krows, lanes]) <= cnt[hd]
                        val = pr[hd] * pairs(qq_ref[hd, krows, lanes])
                        terms.append(jnp.where(sel, val, jnp.zeros_like(val)))
                    while len(terms) > 1:
                        terms = [a + b for a, b in zip(terms[::2], terms[1::2])]
                    r0 = il * nk + 2 * m * sub
                    act = pairs(_pack_pair(_gelu(ht_ref[r0:r0 + sub, lanes]),
                                           _gelu(ht_ref[r0 + sub:r0 + 2 * sub, lanes])))
                    w0 = il * (nk // 2) + m * sub
                    wt_ref[w0:w0 + sub, lanes] = pltpu.bitcast(terms[0] * act, jnp.uint32)

    def project(tok):
        acc_ref[:, tok] += _dot(vt_ref[...], pltpu.bitcast(wt_ref[:, tok], BF16))

    toks = [slice(s * tsub, (s + 1) * tsub) for s in range(tb // tsub)]
    activations(toks[0])
    for s, tok in enumerate(toks):
        if s + 1 < len(toks):
            activations(toks[s + 1])
        weigh(tok)
        project(tok)

    @pl.when(j == pl.num_programs(1) - 1)
    def _():
        res = h_ref[...] + acc_ref[...].T
        if final_norm:
            ms = jnp.mean(res * res, axis=-1, keepdims=True)
            res = res * lax.rsqrt(ms + RMS_EPS) * nw_ref[...]
        o_ref[...] = res


def peer_ffn(xn, u, vt, route, h, norm_w, *, final_norm, tb=1024, eb=1024, tsub=256):
    t, d = h.shape
    ne = u.shape[0]
    nk = PEER_NKEYS
    cnt, rank, pp, qq = route
    second = pl.BlockSpec((PEER_HEADS, nk // 2, tb), lambda i, j: (0, 0, i))
    first = pl.BlockSpec((PEER_HEADS, eb // nk, tb), lambda i, j: (0, j, i))
    return pl.pallas_call(
        functools.partial(_peer_ffn_kernel, final_norm=final_norm, tsub=tsub),
        grid=(t // tb, ne // eb),
        in_specs=[pl.BlockSpec((tb, d), lambda i, j: (i, 0)),
                  pl.BlockSpec((eb, d), lambda i, j: (j, 0)),
                  pl.BlockSpec((d, eb), lambda i, j: (0, j)),
                  first, second, first, second,
                  pl.BlockSpec((tb, d), lambda i, j: (i, 0)),
                  pl.BlockSpec((1, d), lambda i, j: (0, 0))],
        out_specs=pl.BlockSpec((tb, d), lambda i, j: (i, 0)),
        out_shape=jax.ShapeDtypeStruct((t, d), F32),
        scratch_shapes=[pltpu.VMEM((eb, tb), F32), pltpu.VMEM((eb // 2, tb), jnp.uint32),
                        pltpu.VMEM((d, tb), F32)],
        compiler_params=_params("parallel", "arbitrary"),
        name="peer_ffn",
    )(xn, u, vt, cnt, rank, pp, qq, h, norm_w.reshape(1, d).astype(F32))


def _pair_order(v_tab):
    ne, d = v_tab.shape
    sub = SUBLANES
    return v_tab.reshape(ne // (2 * sub), 2, sub, d).transpose(0, 2, 1, 3).reshape(ne, d)


def peer_layer(h, norm_w, w_q, sub_keys, u_tab, v_tab, out_norm_w, *, final_norm):
    q, xn = norm_matmul(h, norm_w, w_q.astype(BF16), tn=1024, emit_xn=True)
    route = peer_route(q, sub_keys.astype(BF16))
    vt = _pair_order(v_tab.astype(BF16)).T
    return peer_ffn(xn, u_tab.astype(BF16), vt, route, h, out_norm_w, final_norm=final_norm)


def kernel(x, norm_mix_w, norm_ffn_w, norm_out_w, ev_in_w, ev_out_w, hgrn_lb, hgrn_gn_w, s5_a_re, s5_a_im, s5_log_dt, s5_b_re, s5_b_im, s5_c_re, s5_c_im, s5_d, s5_glu_w, od_in_w, od_out_w, gdn_conv_w, gdn_a_log, gdn_dt_bias, gdn_gn_w, peer_wq, peer_sub_keys, peer_u, peer_v):
    bsz, seq, d = x.shape
    depth = norm_mix_w.shape[0]
    t = bsz * seq
    h = x.reshape(t, d).astype(F32)
    lb_all = jnp.cumsum(jax.nn.softmax(hgrn_lb.astype(F32), axis=0), axis=0)
    hgrn_w = HGRN_HEADS * HGRN_D
    for layer in range(depth):
        j = layer // 2
        if layer % 2 == 0:
            proj = norm_matmul(h, norm_mix_w[layer], ev_in_w[j].astype(BF16), tn=512)
            o_a = hgrn2(proj, lb_all[j], hgrn_gn_w[j], bsz, seq)
            weights = _s5_weights(s5_a_re[j], s5_a_im[j], s5_log_dt[j], s5_b_re[j], s5_b_im[j],
                                  s5_c_re[j], s5_c_im[j], s5_d[j])
            y4 = s5(proj, 4 * hgrn_w, weights, bsz, seq)
            w_out = ev_out_w[j].astype(BF16)
            h = even_out(h, o_a, y4, s5_glu_w[j].astype(BF16), w_out[:hgrn_w], w_out[hgrn_w:])
        else:
            w_in = od_in_w[j]
            pad = (-w_in.shape[1]) % (3 * LANES)
            w_in = jnp.pad(w_in, ((0, 0), (0, pad))).astype(BF16)
            proj = norm_matmul(h, norm_mix_w[layer], w_in, tn=w_in.shape[1] // 3)
            o_c = gdn(proj, gdn_conv_w[j], gdn_a_log[j], gdn_dt_bias[j], gdn_gn_w[j], bsz, seq)
            h = matmul_residual(h, o_c, od_out_w[j].astype(BF16))
        h = peer_layer(h, norm_ffn_w[layer], peer_wq[layer], peer_sub_keys[layer], peer_u[layer],
                       peer_v[layer], norm_out_w, final_norm=(layer == depth - 1))
    return h.reshape(bsz, seq, d)
```

```python
import functools
import math

import jax
import jax.numpy as jnp
from jax import lax
from jax.experimental import pallas as pl
from jax.experimental.pallas import tpu as pltpu

F32 = jnp.float32
BF16 = jnp.bfloat16
HI = lax.Precision.HIGHEST

RMS_EPS = 1e-6
L2_EPS = 1e-6
LANES = 128
VMEM_LIMIT = 56 * 1024 * 1024

HGRN_HEADS, HGRN_D, HGRN_CHUNK = 4, 128, 32
S5_GROUP, S5_STATE, S5_L = 16, 64, 8
S5_GB = LANES // S5_GROUP
GDN_HEADS, GDN_D, GDN_CONV, GDN_CHUNK = 8, 128, 4, 64
PEER_HEADS, PEER_NKEYS, PEER_DHALF, PEER_TOPK = 8, 128, 128, 16
MASKED = 1e30


def _dot(a, b, precision=None):
    return jnp.dot(a, b, preferred_element_type=F32, precision=precision)


def _dot_nt(a, b):
    return lax.dot_general(a, b, (((1,), (1,)), ((), ())), preferred_element_type=F32)


def _dot_tn(a, b):
    return lax.dot_general(a, b, (((0,), (0,)), ((), ())), preferred_element_type=F32)


def _silu(x):
    return x * jax.nn.sigmoid(x)


def _gelu(x):
    return 0.5 * x * (1.0 + lax.erf(x * (2.0 ** -0.5)))


def _softplus(x):
    return jnp.maximum(x, 0.0) + jnp.log1p(jnp.exp(-jnp.abs(x)))


def _params(*sem):
    return pltpu.CompilerParams(dimension_semantics=sem, vmem_limit_bytes=VMEM_LIMIT)


def _tril(n, strict=False):
    r = lax.broadcasted_iota(jnp.int32, (n, n), 0)
    c = lax.broadcasted_iota(jnp.int32, (n, n), 1)
    return (r > c) if strict else (r >= c)


def _norm_matmul_kernel(emit_xn, x_ref, nw_ref, w_ref, o_ref, *rest):
    x = x_ref[...]
    ms = jnp.mean(x * x, axis=-1, keepdims=True)
    xn = (x * lax.rsqrt(ms + RMS_EPS) * nw_ref[...]).astype(BF16)
    if emit_xn:
        rest[0][...] = xn
    o_ref[...] = _dot(xn, w_ref[...])


def norm_matmul(h, nw, w, *, emit_xn=False, tm=512):
    t, d = h.shape
    n = w.shape[1]
    assert t % tm == 0 and n % LANES == 0
    out_shape = [jax.ShapeDtypeStruct((t, n), F32)]
    out_specs = [pl.BlockSpec((tm, n), lambda i: (i, 0))]
    if emit_xn:
        out_shape.append(jax.ShapeDtypeStruct((t, d), BF16))
        out_specs.append(pl.BlockSpec((tm, d), lambda i: (i, 0)))
    res = pl.pallas_call(
        functools.partial(_norm_matmul_kernel, emit_xn),
        grid=(t // tm,),
        in_specs=[pl.BlockSpec((tm, d), lambda i: (i, 0)),
                  pl.BlockSpec((1, d), lambda i: (0, 0)),
                  pl.BlockSpec((d, n), lambda i: (0, 0))],
        out_specs=out_specs,
        out_shape=out_shape,
        compiler_params=_params("parallel"),
        name="norm_matmul",
    )(h, nw.reshape(1, d).astype(F32), w)
    return res if emit_xn else res[0]


def _hgrn_kernel(q_ref, f_ref, i_ref, g_ref, lb_ref, gn_ref, o_ref, st_ref, *, tb):
    c, d, nh = HGRN_CHUNK, HGRN_D, HGRN_HEADS
    nc = tb // c

    @pl.when(pl.program_id(1) == 0)
    def _():
        st_ref[...] = jnp.zeros_like(st_ref)

    lb = lb_ref[...]
    q = _silu(q_ref[...]) * (d ** -0.5)
    f = lb + (1.0 - lb) * jax.nn.sigmoid(f_ref[...])
    k = 1.0 - f
    logf = jnp.log(f)
    v = i_ref[...]
    tri = _tril(c).astype(F32)
    causal = _tril(c)

    units = [(h, n) for n in range(nc) for h in range(nh)]
    block = lambda x, h, n: x[n * c:(n + 1) * c, h * d:(h + 1) * d]
    cum = {u: _dot(tri, block(logf, *u), HI) for u in units}
    qe, kv, intra, tot = {}, {}, {}, {}
    for u in units:
        last = cum[u][c - 1:c]
        qe[u] = (block(q, *u) * jnp.exp(cum[u])).astype(BF16)
        ke = (block(k, *u) * jnp.exp(-cum[u])).astype(BF16)
        kd = (block(k, *u) * jnp.exp(last - cum[u])).astype(BF16)
        vb = block(v, *u).astype(BF16)
        att = jnp.where(causal, _dot_nt(qe[u], ke), 0.0)
        intra[u] = _dot(att.astype(BF16), vb)
        kv[u] = _dot_tn(vb, kd)
        tot[u] = jnp.exp(last)
    st = [st_ref[h] for h in range(nh)]
    outs = {}
    for n in range(nc):
        for h in range(nh):
            outs[h, n] = intra[h, n] + _dot_nt(qe[h, n], st[h].astype(BF16))
        st = [tot[h, n] * st[h] + kv[h, n] for h in range(nh)]
    for h in range(nh):
        lanes = slice(h * d, (h + 1) * d)
        st_ref[h] = st[h]
        o = jnp.concatenate([outs[h, n] for n in range(nc)], axis=0)
        ms = jnp.mean(o * o, axis=-1, keepdims=True)
        o = o * lax.rsqrt(ms + RMS_EPS) * gn_ref[...]
        o_ref[:, lanes] = o * _silu(g_ref[:, lanes])


def hgrn2(proj, lb, gn_w, bsz, seq, *, tb=256):
    nh, d = HGRN_HEADS, HGRN_D
    width = nh * d
    ns = seq // tb
    col = lambda k: (lambda b, s: (b * ns + s, k))
    return pl.pallas_call(
        functools.partial(_hgrn_kernel, tb=tb),
        grid=(bsz, ns),
        in_specs=[pl.BlockSpec((tb, width), col(0)),
                  pl.BlockSpec((tb, width), col(1)),
                  pl.BlockSpec((tb, width), col(2)),
                  pl.BlockSpec((tb, width), col(3)),
                  pl.BlockSpec((1, width), lambda b, s: (0, 0)),
                  pl.BlockSpec((1, d), lambda b, s: (0, 0))],
        out_specs=pl.BlockSpec((tb, width), col(0)),
        out_shape=jax.ShapeDtypeStruct((bsz * seq, width), F32),
        scratch_shapes=[pltpu.VMEM((nh, d, d), F32)],
        compiler_params=_params("parallel", "arbitrary"),
        name="hgrn2",
    )(proj, proj, proj, proj, lb.reshape(1, width).astype(F32), gn_w.reshape(1, d).astype(F32))


def _s5_weights(a_re, a_im, log_dt, b_re, b_im, c_re, c_im, d_skip):
    ln, gb = S5_L, S5_GB
    g, n = a_re.shape
    p = b_re.shape[-1]
    nb = g // gb
    a_re, a_im, b_re, b_im, c_re, c_im = (t.astype(F32) for t in (a_re, a_im, b_re, b_im, c_re, c_im))
    dt = jnp.exp(log_dt.astype(F32))[:, None]
    lam_re, lam_im = a_re * dt, a_im * dt
    def powers(exponents):
        kk = jnp.asarray(exponents, F32)[:, None, None]
        mag = jnp.exp(kk * lam_re)
        return mag * jnp.cos(kk * lam_im), mag * jnp.sin(kk * lam_im)

    pw_re, pw_im = powers(range(ln + 1))
    abar_re, abar_im = pw_re[1], pw_im[1]
    den = a_re * a_re + a_im * a_im
    coef_re = ((abar_re - 1.0) * a_re + abar_im * a_im) / den
    coef_im = (abar_im * a_re - (abar_re - 1.0) * a_im) / den
    bb_re = coef_re[..., None] * b_re - coef_im[..., None] * b_im
    bb_im = coef_re[..., None] * b_im + coef_im[..., None] * b_re
    eye = jnp.eye(gb, dtype=F32)

    rv, iv = powers(range(ln - 1, -1, -1))
    t1_re = rv[..., None] * bb_re - iv[..., None] * bb_im
    t1_im = rv[..., None] * bb_im + iv[..., None] * bb_re

    def expand_b(t1):
        w = jnp.einsum('tbgnp,gh->btgphn', t1.reshape(ln, nb, gb, n, p), eye)
        return w.reshape(nb, ln * gb * p, gb * n)

    w_b = jnp.concatenate([expand_b(t1_re), expand_b(t1_im)], axis=-1)

    qr, qi = pw_re[1:], pw_im[1:]
    t2_re = c_re[None] * qr[:, :, None, :] - c_im[None] * qi[:, :, None, :]
    t2_im = c_re[None] * qi[:, :, None, :] + c_im[None] * qr[:, :, None, :]

    def expand_c(t2):
        w = jnp.einsum('tbgpn,gh->bgnthp', t2.reshape(ln, nb, gb, p, n), eye)
        return w.reshape(nb, gb * n, ln * gb * p)

    w_c = jnp.concatenate([expand_c(t2_re), -expand_c(t2_im)], axis=1)

    cr = c_re[None] * pw_re[:ln, :, None, :] - c_im[None] * pw_im[:ln, :, None, :]
    ci = c_re[None] * pw_im[:ln, :, None, :] + c_im[None] * pw_re[:ln, :, None, :]
    kd = jnp.einsum('dgpn,gnq->dgpq', cr, bb_re) - jnp.einsum('dgpn,gnq->dgpq', ci, bb_im)
    tau = jnp.arange(ln)
    delta = tau[None, :] - tau[:, None]
    kt = jnp.where((delta >= 0)[:, :, None, None, None], kd[jnp.clip(delta, 0)], 0.0)
    w_k = jnp.einsum('stbgpq,gh->bsgqthp', kt.reshape(ln, ln, nb, gb, p, p), eye)
    w_k = w_k.reshape(nb, ln * gb * p, ln * gb * p)

    al = jnp.stack([pw_re[ln].reshape(nb, 1, gb * n), pw_im[ln].reshape(nb, 1, gb * n)], axis=1)
    dd = jnp.tile(d_skip.astype(F32).reshape(nb, 1, gb * p), (1, 1, ln))
    return w_b.astype(BF16), w_k.astype(BF16), w_c.astype(BF16), al.reshape(nb, 2, gb * n), dd


def _s5_kernel(u_ref, wb_ref, wk_ref, wc_ref, al_ref, dd_ref, o_ref, s_ref, xp_ref, st_ref, *, cb):
    ln = S5_L
    half = st_ref.shape[1] // 2

    @pl.when(pl.program_id(2) == 0)
    def _():
        st_ref[...] = jnp.zeros_like(st_ref)

    u = jnp.concatenate([u_ref[pl.ds(tau, cb, stride=ln), :] for tau in range(ln)], axis=1)
    ub = u.astype(BF16)
    s_ref[...] = _dot(ub, wb_ref[...])
    ar = al_ref[0:1, :]
    ai = al_ref[1:2, :]

    def step(c, carry):
        xr, xi = carry
        xp_ref[pl.ds(c, 1), :] = jnp.concatenate([xr, xi], axis=1)
        srow = s_ref[pl.ds(c, 1), :]
        nr = ar * xr - ai * xi + srow[:, :half]
        ni = ar * xi + ai * xr + srow[:, half:]
        return nr, ni

    x0 = st_ref[...]
    xr, xi = lax.fori_loop(0, cb, step, (x0[:, :half], x0[:, half:]), unroll=8)
    st_ref[...] = jnp.concatenate([xr, xi], axis=1)
    y = _gelu(_dot(ub, wk_ref[...]) + _dot(xp_ref[...].astype(BF16), wc_ref[...]) + dd_ref[...] * u)
    for tau in range(ln):
        o_ref[pl.ds(tau, cb, stride=ln), :] = y[:, tau * LANES:(tau + 1) * LANES]


def s5(proj, col0, weights, bsz, seq, *, cb=256):
    ln = S5_L
    w_b, w_k, w_c, al, dd = weights
    nb = w_b.shape[0]
    t, width = proj.shape
    ncb = seq // ln // cb
    assert width % LANES == 0 and col0 % LANES == 0 and ncb * cb * ln == seq
    kw, ks = w_b.shape[1], w_b.shape[2]

    return pl.pallas_call(
        functools.partial(_s5_kernel, cb=cb),
        grid=(nb, bsz, ncb),
        in_specs=[
            pl.BlockSpec((cb * ln, LANES), lambda g, b, c: (b * ncb + c, col0 // LANES + g)),
            pl.BlockSpec((None, kw, ks), lambda g, b, c: (g, 0, 0)),
            pl.BlockSpec((None, kw, kw), lambda g, b, c: (g, 0, 0)),
            pl.BlockSpec((None, ks, kw), lambda g, b, c: (g, 0, 0)),
            pl.BlockSpec((None, 2, ks // 2), lambda g, b, c: (g, 0, 0)),
            pl.BlockSpec((None, 1, kw), lambda g, b, c: (g, 0, 0))],
        out_specs=pl.BlockSpec((None, cb * ln, LANES), lambda g, b, c: (g, b * ncb + c, 0)),
        out_shape=jax.ShapeDtypeStruct((nb, t, LANES), F32),
        scratch_shapes=[pltpu.VMEM((cb, ks), F32), pltpu.VMEM((cb, ks), F32), pltpu.VMEM((1, ks), F32)],
        compiler_params=_params("parallel", "parallel", "arbitrary"),
        name="s5",
    )(proj, w_b, w_k, w_c, al, dd)


def _even_out_kernel(h_ref, oa_ref, y_ref, glu_ref, wa_ref, wb_ref, o_ref):
    y = jnp.concatenate([y_ref[i] for i in range(y_ref.shape[0])], axis=1)
    ob = y * jax.nn.sigmoid(_dot(y.astype(BF16), glu_ref[...]))
    o_ref[...] = h_ref[...] + _dot(oa_ref[...].astype(BF16), wa_ref[...]) + _dot(ob.astype(BF16), wb_ref[...])


def even_out(h, oa, y4, glu_w, w_a, w_b, *, tm=512):
    t, d = h.shape
    wa_rows, wb_rows = w_a.shape[0], w_b.shape[0]
    nb = y4.shape[0]
    return pl.pallas_call(
        _even_out_kernel,
        grid=(t // tm,),
        in_specs=[pl.BlockSpec((tm, d), lambda i: (i, 0)),
                  pl.BlockSpec((tm, wa_rows), lambda i: (i, 0)),
                  pl.BlockSpec((nb, tm, LANES), lambda i: (0, i, 0)),
                  pl.BlockSpec((wb_rows, wb_rows), lambda i: (0, 0)),
                  pl.BlockSpec((wa_rows, d), lambda i: (0, 0)),
                  pl.BlockSpec((wb_rows, d), lambda i: (0, 0))],
        out_specs=pl.BlockSpec((tm, d), lambda i: (i, 0)),
        out_shape=jax.ShapeDtypeStruct((t, d), F32),
        compiler_params=_params("parallel"),
        name="even_out",
    )(h, oa, y4, glu_w, w_a, w_b)


def _matmul_res_kernel(h_ref, a_ref, w_ref, o_ref):
    o_ref[...] = h_ref[...] + _dot(a_ref[...].astype(BF16), w_ref[...])


def matmul_residual(h, a, w, *, tm=512):
    t, d = h.shape
    k = a.shape[1]
    return pl.pallas_call(
        _matmul_res_kernel,
        grid=(t // tm,),
        in_specs=[pl.BlockSpec((tm, d), lambda i: (i, 0)),
                  pl.BlockSpec((tm, k), lambda i: (i, 0)),
                  pl.BlockSpec((k, d), lambda i: (0, 0))],
        out_specs=pl.BlockSpec((tm, d), lambda i: (i, 0)),
        out_shape=jax.ShapeDtypeStruct((t, d), F32),
        compiler_params=_params("parallel"),
        name="matmul_residual",
    )(h, a, w)


def _split_bf16(x):
    hi = x.astype(BF16)
    return hi, (x - hi.astype(F32)).astype(BF16)


def _dot_split(xs, ys):
    (xh, xl), (yh, yl) = xs, ys
    return _dot(xh, yh) + (_dot(xh, yl) + _dot(xl, yh))


def _gdn_kernel(q_ref, k_ref, v_ref, z_ref, ba_ref, cw_ref, nega_ref, dtb_ref, gn_ref, o_ref,
                st_ref, cb_ref, *, tb):
    c, d, nh = GDN_CHUNK, GDN_D, GDN_HEADS
    width = nh * d
    hist = 8

    @pl.when(pl.program_id(1) == 0)
    def _():
        st_ref[...] = jnp.zeros_like(st_ref)
        cb_ref[:, 0:hist, :] = jnp.zeros((3, hist, width), F32)

    cb_ref[0, hist:hist + tb, :] = q_ref[...]
    cb_ref[1, hist:hist + tb, :] = k_ref[...]
    cb_ref[2, hist:hist + tb, :] = v_ref[...]

    def conv(idx, lanes):
        base = hist - (GDN_CONV - 1)
        col = slice(idx * width + lanes.start, idx * width + lanes.stop)
        acc = cw_ref[0:1, col] * cb_ref[idx, base:base + tb, lanes]
        for j in range(1, GDN_CONV):
            acc = acc + cw_ref[j:j + 1, col] * cb_ref[idx, base + j:base + j + tb, lanes]
        return _silu(acc)

    ba = ba_ref[...]
    beta_all = jax.nn.sigmoid(ba)
    g_all = nega_ref[...] * _softplus(ba + dtb_ref[...])

    incl = _tril(c)
    strict = _tril(c, strict=True)
    tri = incl.astype(F32)
    eye = (lax.broadcasted_iota(jnp.int32, (c, c), 0) == lax.broadcasted_iota(jnp.int32, (c, c), 1)).astype(F32)
    decay = [_dot(tri, g_all[n * c:(n + 1) * c], HI) for n in range(tb // c)]
    decay_t = [dc.T for dc in decay]

    nc = tb // c
    units = [(h, n) for n in range(nc) for h in range(nh)]
    prep = {}
    for h in range(nh):
        lanes = slice(h * d, (h + 1) * d)
        qc, kc, vc = conv(0, lanes), conv(1, lanes), conv(2, lanes)
        q = qc * lax.rsqrt(jnp.sum(qc * qc, axis=-1, keepdims=True) + L2_EPS) * (d ** -0.5)
        k = kc * lax.rsqrt(jnp.sum(kc * kc, axis=-1, keepdims=True) + L2_EPS)
        for n in range(nc):
            sl = slice(n * c, (n + 1) * c)
            dcol = jnp.broadcast_to(decay[n][:, nh + h:nh + h + 1], (c, d))
            diff = dcol[:, :c] - decay_t[n][nh + h:nh + h + 1, :]
            lmask = jnp.where(incl, jnp.exp(jnp.where(incl, diff, 0.0)), 0.0)
            bc = jnp.broadcast_to(beta_all[sl, h:h + 1], (c, d))
            prep[h, n] = dict(dcol=dcol, lmask=lmask, bc=bc, k=k[sl], q=q[sl], v=vc[sl])
    cb_ref[:, 0:hist, :] = cb_ref[:, tb:tb + hist, :]

    xs, t_inv = {}, {}
    for u in units:
        p = prep[u]
        p['kb'] = p['k'] * p['bc']
        p['knb'] = p['k'].astype(BF16)
        a_mat = jnp.where(strict, _dot_nt(p['kb'].astype(BF16), p['knb']) * p['lmask'], 0.0)
        t_inv[u] = eye - a_mat
        xs[u] = _split_bf16(-a_mat)
    for _ in range(int(math.log2(c)) - 1):
        for u in units:
            xs[u] = _split_bf16(_dot_split(xs[u], xs[u]))
        for u in units:
            t_inv[u] = t_inv[u] + _dot_split(_split_bf16(t_inv[u]), xs[u])
    for u in units:
        p = prep[u]
        tib = t_inv[u].astype(BF16)
        edc = jnp.exp(p['dcol'])
        last = p['dcol'][c - 1:c]
        p['value'] = _dot(tib, (p['v'] * p['bc']).astype(BF16))
        p['kcum'] = _dot(tib, (p['kb'] * edc).astype(BF16)).astype(BF16)
        p['attn'] = (_dot_nt(p['q'].astype(BF16), p['knb']) * p['lmask']).astype(BF16)
        p['qdec'] = (p['q'] * edc).astype(BF16)
        p['ktail'] = (p['k'] * jnp.exp(last - p['dcol'])).astype(BF16)
        p['tot'] = jnp.exp(last)

    st = [st_ref[h] for h in range(nh)]
    outs = {}
    for n in range(nc):
        sb = [s.astype(BF16) for s in st]
        vnb = [(prep[h, n]['value'] - _dot(prep[h, n]['kcum'], sb[h])).astype(BF16) for h in range(nh)]
        for h in range(nh):
            p = prep[h, n]
            outs[h, n] = _dot(p['qdec'], sb[h]) + _dot(p['attn'], vnb[h])
        st = [prep[h, n]['tot'] * st[h] + _dot_tn(prep[h, n]['ktail'], vnb[h]) for h in range(nh)]
    for h in range(nh):
        lanes = slice(h * d, (h + 1) * d)
        st_ref[h] = st[h]
        o = jnp.concatenate([outs[h, n] for n in range(nc)], axis=0)
        ms = jnp.mean(o * o, axis=-1, keepdims=True)
        o = o * lax.rsqrt(ms + RMS_EPS) * gn_ref[...]
        o_ref[:, lanes] = o * _silu(z_ref[:, lanes])


def gdn(proj, conv_w, a_log, dt_bias, gn_w, bsz, seq, *, tb=128):
    nh, d = GDN_HEADS, GDN_D
    width = nh * d
    ns = seq // tb
    row = lambda b, s: b * ns + s
    lane = jnp.arange(LANES)
    in_a = (lane >= nh) & (lane < 2 * nh)
    idx = jnp.clip(lane - nh, 0, nh - 1)
    nega = jnp.where(in_a, -jnp.exp(a_log.astype(F32))[idx], 0.0).reshape(1, LANES)
    dtb = jnp.where(in_a, dt_bias.astype(F32)[idx], 0.0).reshape(1, LANES)
    const = lambda b, s: (0, 0)
    return pl.pallas_call(
        functools.partial(_gdn_kernel, tb=tb),
        grid=(bsz, ns),
        in_specs=[pl.BlockSpec((tb, width), lambda b, s: (row(b, s), 0)),
                  pl.BlockSpec((tb, width), lambda b, s: (row(b, s), 1)),
                  pl.BlockSpec((tb, width), lambda b, s: (row(b, s), 2)),
                  pl.BlockSpec((tb, width), lambda b, s: (row(b, s), 3)),
                  pl.BlockSpec((tb, LANES), lambda b, s: (row(b, s), 4 * nh)),
                  pl.BlockSpec((GDN_CONV, 3 * width), const),
                  pl.BlockSpec((1, LANES), const),
                  pl.BlockSpec((1, LANES), const),
                  pl.BlockSpec((1, d), const)],
        out_specs=pl.BlockSpec((tb, width), lambda b, s: (row(b, s), 0)),
        out_shape=jax.ShapeDtypeStruct((bsz * seq, width), F32),
        scratch_shapes=[pltpu.VMEM((nh, d, d), F32), pltpu.VMEM((3, tb + 8, width), F32)],
        compiler_params=_params("parallel", "arbitrary"),
        name="gdn",
    )(proj, proj, proj, proj, proj, conv_w.astype(F32), nega, dtb, gn_w.reshape(1, d).astype(F32))


NO_RANK = 100.0
SUBLANES = 8


def _pack_pair(lo, hi):
    lo_bits = pltpu.bitcast(lo.astype(BF16).astype(F32), jnp.uint32)
    hi_bits = pltpu.bitcast(hi.astype(BF16).astype(F32), jnp.uint32)
    return hi_bits | (lo_bits >> 16)


def _pack_rows(x):
    sub = SUBLANES
    return jnp.concatenate([_pack_pair(x[2 * m * sub:(2 * m + 1) * sub], x[(2 * m + 1) * sub:(2 * m + 2) * sub])
                            for m in range(x.shape[0] // (2 * sub))], axis=0)


def _ranked_top(s, count, with_rank):
    work, vals = s, []
    rank = jnp.full(s.shape, NO_RANK, F32) if with_rank else None
    for r in range(count):
        m = jnp.max(work, axis=0, keepdims=True)
        hit = work == m
        vals.append(m)
        if with_rank:
            rank = jnp.where(hit, float(r + 1), rank)
        work = jnp.where(hit, -MASKED, work)
    return vals, rank


def _route_kernel(q_ref, keys_ref, cnt_ref, rank_ref, p_ref, qq_ref):
    kk = PEER_TOPK
    sub = SUBLANES
    for h in range(PEER_HEADS):
        sc, top, rank = [], [], []
        for c in range(2):
            lo = (2 * h + c) * PEER_DHALF
            qh = q_ref[:, lo:lo + PEER_DHALF].astype(BF16)
            s = _dot_nt(keys_ref[c], qh)
            vals, rk = _ranked_top(s, kk, with_rank=(c == 1))
            sc.append(s)
            top.append(vals)
            rank.append(rk)
        v1 = jnp.concatenate(top[0], axis=0)
        v2 = jnp.concatenate(top[1], axis=0)
        cands = [top[0][0] + v2[:sub], top[0][0] + v2[sub:]]
        cands += [top[0][a] + v2[:sub] for a in range(1, sub)]
        cands += [v1[sub:] + top[1][0]]
        work = list(cands)
        best = []
        for _ in range(kk + 1):
            m = jnp.max(functools.reduce(jnp.maximum, work), axis=0, keepdims=True)
            best.append(m)
            work = [jnp.where(w == m, -MASKED, w) for w in work]
        theta = 0.5 * (best[kk - 1] + best[kk])
        cmax = best[0]
        taken = [cd >= theta for cd in cands]
        z = functools.reduce(jnp.add, [jnp.sum(jnp.where(tk, jnp.exp(cd - cmax), 0.0), axis=0, keepdims=True)
                                       for tk, cd in zip(taken, cands)])
        ones = [jnp.where(tk, 1.0, 0.0) for tk in taken]
        per_row = [jnp.sum(o, axis=0, keepdims=True) for o in ones[:sub + 1]]
        cnt_rows = [per_row[0] + per_row[1]] + per_row[2:] + [ones[sub + 1][r:r + 1] for r in range(kk - sub)]
        cnt = jnp.zeros_like(sc[0])
        for a in range(kk):
            cnt = jnp.where(sc[0] == top[0][a], cnt_rows[a], cnt)
        pp = jnp.where(sc[0] >= top[0][kk - 1], jnp.exp(sc[0] - top[0][0]), 0.0) / z
        qq = jnp.where(rank[1] < NO_RANK, jnp.exp(sc[1] - top[1][0]), 0.0)
        cnt_ref[h] = _pack_pair(cnt, cnt)
        p_ref[h] = _pack_pair(pp, pp)
        rank_ref[h] = _pack_rows(rank[1])
        qq_ref[h] = _pack_rows(qq)


def peer_route(q, keys, *, tbk=256):
    t = q.shape[0]
    nk = PEER_NKEYS
    first = pl.BlockSpec((PEER_HEADS, nk, tbk), lambda i: (0, 0, i))
    second = pl.BlockSpec((PEER_HEADS, nk // 2, tbk), lambda i: (0, 0, i))
    shape = lambda rows: jax.ShapeDtypeStruct((PEER_HEADS, rows, t), jnp.uint32)
    return pl.pallas_call(
        _route_kernel,
        grid=(t // tbk,),
        in_specs=[pl.BlockSpec((tbk, q.shape[1]), lambda i: (i, 0)),
                  pl.BlockSpec((2, nk, PEER_DHALF), lambda i: (0, 0, 0))],
        out_specs=[first, second, first, second],
        out_shape=[shape(nk), shape(nk // 2), shape(nk), shape(nk // 2)],
        compiler_params=_params("parallel"),
        name="peer_route",
    )(q, keys)


def _peer_ffn_kernel(x_ref, u_ref, vt_ref, cnt_ref, rank_ref, p_ref, qq_ref, h_ref, nw_ref, o_ref,
                     *scratch, final_norm, tsub):
    nsub = len(scratch) // 3
    ht_refs, wt_refs, acc_refs = scratch[:nsub], scratch[nsub:2 * nsub], scratch[2 * nsub:]
    eb = ht_refs[0].shape[0]
    nk = PEER_NKEYS
    sub = SUBLANES
    j = pl.program_id(1)

    @pl.when(j == 0)
    def _():
        for acc_ref in acc_refs:
            acc_ref[...] = jnp.zeros_like(acc_ref)

    def pairs(words):
        return pltpu.bitcast(words, BF16)

    def activations(s):
        ht_refs[s][...] = _dot_nt(u_ref[...], x_ref[s * tsub:(s + 1) * tsub, :])

    def weigh(s):
        ht_ref, wt_ref = ht_refs[s], wt_refs[s]
        for tl in range(tsub // LANES):
            lanes = slice(s * tsub + tl * LANES, s * tsub + (tl + 1) * LANES)
            local = slice(tl * LANES, (tl + 1) * LANES)
            for il in range(eb // nk):
                cnt = [pairs(jnp.broadcast_to(cnt_ref[hd, il:il + 1, lanes], (sub, LANES)))
                       for hd in range(PEER_HEADS)]
                pr = [pairs(jnp.broadcast_to(p_ref[hd, il:il + 1, lanes], (sub, LANES)))
                      for hd in range(PEER_HEADS)]
                for m in range(nk // (2 * sub)):
                    krows = slice(m * sub, (m + 1) * sub)
                    terms = []
                    for hd in range(PEER_HEADS):
                        sel = pairs(rank_ref[hd, krows, lanes]) <= cnt[hd]
                        val = pr[hd] * pairs(qq_ref[hd, krows, lanes])
                        terms.append(jnp.where(sel, val, jnp.zeros_like(val)))
                    while len(terms) > 1:
                        terms = [a + b for a, b in zip(terms[::2], terms[1::2])]
                    r0 = il * nk + 2 * m * sub
                    act = pairs(_pack_pair(_gelu(ht_ref[r0:r0 + sub, local]),
                                           _gelu(ht_ref[r0 + sub:r0 + 2 * sub, local])))
                    w0 = il * (nk // 2) + m * sub
                    wt_ref[w0:w0 + sub, local] = pltpu.bitcast(terms[0] * act, jnp.uint32)

    def project(s):
        acc_refs[s][...] += _dot(vt_ref[...], pltpu.bitcast(wt_refs[s][...], BF16))

    activations(0)
    for s in range(nsub):
        if s + 1 < nsub:
            activations(s + 1)
        weigh(s)
        project(s)

    @pl.when(j == pl.num_programs(1) - 1)
    def _():
        res = h_ref[...] + jnp.concatenate([acc_ref[...] for acc_ref in acc_refs], axis=1).T
        if final_norm:
            ms = jnp.mean(res * res, axis=-1, keepdims=True)
            res = res * lax.rsqrt(ms + RMS_EPS) * nw_ref[...]
        o_ref[...] = res


def peer_ffn(xn, u, vt, route, h, norm_w, *, final_norm, tb=1024, eb=1024, tsub=256):
    t, d = h.shape
    ne = u.shape[0]
    nk = PEER_NKEYS
    cnt, rank, pp, qq = route
    second = pl.BlockSpec((PEER_HEADS, nk // 2, tb), lambda i, j: (0, 0, i))
    first = pl.BlockSpec((PEER_HEADS, eb // nk, tb), lambda i, j: (0, j, i))
    return pl.pallas_call(
        functools.partial(_peer_ffn_kernel, final_norm=final_norm, tsub=tsub),
        grid=(t // tb, ne // eb),
        in_specs=[pl.BlockSpec((tb, d), lambda i, j: (i, 0)),
                  pl.BlockSpec((eb, d), lambda i, j: (j, 0)),
                  pl.BlockSpec((d, eb), lambda i, j: (0, j)),
                  first, second, first, second,
                  pl.BlockSpec((tb, d), lambda i, j: (i, 0)),
                  pl.BlockSpec((1, d), lambda i, j: (0, 0))],
        out_specs=pl.BlockSpec((tb, d), lambda i, j: (i, 0)),
        out_shape=jax.ShapeDtypeStruct((t, d), F32),
        scratch_shapes=([pltpu.VMEM((eb, tsub), F32)] * (tb // tsub)
                        + [pltpu.VMEM((eb // 2, tsub), jnp.uint32)] * (tb // tsub)
                        + [pltpu.VMEM((d, tsub), F32)] * (tb // tsub)),
        compiler_params=_params("parallel", "arbitrary"),
        name="peer_ffn",
    )(xn, u, vt, cnt, rank, pp, qq, h, norm_w.reshape(1, d).astype(F32))


def _pair_order(v_tab):
    ne, d = v_tab.shape
    sub = SUBLANES
    return v_tab.reshape(ne // (2 * sub), 2, sub, d).transpose(0, 2, 1, 3).reshape(ne, d)


def peer_layer(h, norm_w, w_q, sub_keys, u_tab, v_tab, out_norm_w, *, final_norm):
    q, xn = norm_matmul(h, norm_w, w_q.astype(BF16), emit_xn=True)
    route = peer_route(q, sub_keys.astype(BF16))
    vt = _pair_order(v_tab.astype(BF16)).T
    return peer_ffn(xn, u_tab.astype(BF16), vt, route, h, out_norm_w, final_norm=final_norm)


def kernel(x, norm_mix_w, norm_ffn_w, norm_out_w, ev_in_w, ev_out_w, hgrn_lb, hgrn_gn_w, s5_a_re, s5_a_im, s5_log_dt, s5_b_re, s5_b_im, s5_c_re, s5_c_im, s5_d, s5_glu_w, od_in_w, od_out_w, gdn_conv_w, gdn_a_log, gdn_dt_bias, gdn_gn_w, peer_wq, peer_sub_keys, peer_u, peer_v):
    bsz, seq, d = x.shape
    depth = norm_mix_w.shape[0]
    t = bsz * seq
    h = x.reshape(t, d).astype(F32)
    lb_all = jnp.cumsum(jax.nn.softmax(hgrn_lb.astype(F32), axis=0), axis=0)
    hgrn_w = HGRN_HEADS * HGRN_D
    for layer in range(depth):
        j = layer // 2
        if layer % 2 == 0:
            proj = norm_matmul(h, norm_mix_w[layer], ev_in_w[j].astype(BF16))
            o_a = hgrn2(proj, lb_all[j], hgrn_gn_w[j], bsz, seq)
            weights = _s5_weights(s5_a_re[j], s5_a_im[j], s5_log_dt[j], s5_b_re[j], s5_b_im[j],
                                  s5_c_re[j], s5_c_im[j], s5_d[j])
            y4 = s5(proj, 4 * hgrn_w, weights, bsz, seq)
            w_out = ev_out_w[j].astype(BF16)
            h = even_out(h, o_a, y4, s5_glu_w[j].astype(BF16), w_out[:hgrn_w], w_out[hgrn_w:])
        else:
            w_in = od_in_w[j]
            pad = (-w_in.shape[1]) % LANES
            w_in = jnp.pad(w_in, ((0, 0), (0, pad))).astype(BF16)
            proj = norm_matmul(h, norm_mix_w[layer], w_in)
            o_c = gdn(proj, gdn_conv_w[j], gdn_a_log[j], gdn_dt_bias[j], gdn_gn_w[j], bsz, seq)
            h = matmul_residual(h, o_c, od_out_w[j].astype(BF16))
        h = peer_layer(h, norm_ffn_w[layer], peer_wq[layer], peer_sub_keys[layer], peer_u[layer],
                       peer_v[layer], norm_out_w, final_norm=(layer == depth - 1))
    return h.reshape(bsz, seq, d)
```

```python
import functools
import math

import jax
import jax.numpy as jnp
from jax import lax
from jax.experimental import pallas as pl
from jax.experimental.pallas import tpu as pltpu

F32 = jnp.float32
BF16 = jnp.bfloat16
HI = lax.Precision.HIGHEST

RMS_EPS = 1e-6
L2_EPS = 1e-6
LANES = 128
VMEM_LIMIT = 56 * 1024 * 1024

HGRN_HEADS, HGRN_D, HGRN_CHUNK = 4, 128, 32
S5_GROUP, S5_STATE, S5_L = 16, 64, 8
S5_GB = LANES // S5_GROUP
GDN_HEADS, GDN_D, GDN_CONV, GDN_CHUNK = 8, 128, 4, 64
PEER_HEADS, PEER_NKEYS, PEER_DHALF, PEER_TOPK = 8, 128, 128, 16
MASKED = 1e30


def _dot(a, b, precision=None):
    return jnp.dot(a, b, preferred_element_type=F32, precision=precision)


def _dot_nt(a, b):
    return lax.dot_general(a, b, (((1,), (1,)), ((), ())), preferred_element_type=F32)


def _dot_tn(a, b):
    return lax.dot_general(a, b, (((0,), (0,)), ((), ())), preferred_element_type=F32)


def _silu(x):
    return x * jax.nn.sigmoid(x)


def _gelu(x):
    return 0.5 * x * (1.0 + lax.erf(x * (2.0 ** -0.5)))


def _softplus(x):
    return jnp.maximum(x, 0.0) + jnp.log1p(jnp.exp(-jnp.abs(x)))


def _params(*sem):
    return pltpu.CompilerParams(dimension_semantics=sem, vmem_limit_bytes=VMEM_LIMIT)


def _tril(n, strict=False):
    r = lax.broadcasted_iota(jnp.int32, (n, n), 0)
    c = lax.broadcasted_iota(jnp.int32, (n, n), 1)
    return (r > c) if strict else (r >= c)


def _norm_matmul_kernel(emit_xn, x_ref, nw_ref, w_ref, o_ref, *rest):
    x = x_ref[...]
    ms = jnp.mean(x * x, axis=-1, keepdims=True)
    xn = (x * lax.rsqrt(ms + RMS_EPS) * nw_ref[...]).astype(BF16)
    if emit_xn:
        rest[0][...] = xn
    o_ref[...] = _dot(xn, w_ref[...])


def norm_matmul(h, nw, w, *, emit_xn=False, tm=512):
    t, d = h.shape
    n = w.shape[1]
    assert t % tm == 0 and n % LANES == 0
    out_shape = [jax.ShapeDtypeStruct((t, n), F32)]
    out_specs = [pl.BlockSpec((tm, n), lambda i: (i, 0))]
    if emit_xn:
        out_shape.append(jax.ShapeDtypeStruct((t, d), BF16))
        out_specs.append(pl.BlockSpec((tm, d), lambda i: (i, 0)))
    res = pl.pallas_call(
        functools.partial(_norm_matmul_kernel, emit_xn),
        grid=(t // tm,),
        in_specs=[pl.BlockSpec((tm, d), lambda i: (i, 0)),
                  pl.BlockSpec((1, d), lambda i: (0, 0)),
                  pl.BlockSpec((d, n), lambda i: (0, 0))],
        out_specs=out_specs,
        out_shape=out_shape,
        compiler_params=_params("parallel"),
        name="norm_matmul",
    )(h, nw.reshape(1, d).astype(F32), w)
    return res if emit_xn else res[0]


def _hgrn_kernel(q_ref, f_ref, i_ref, g_ref, lb_ref, gn_ref, o_ref, st_ref, *, tb):
    c, d, nh = HGRN_CHUNK, HGRN_D, HGRN_HEADS
    nc = tb // c

    @pl.when(pl.program_id(1) == 0)
    def _():
        st_ref[...] = jnp.zeros_like(st_ref)

    lb = lb_ref[...]
    q = _silu(q_ref[...]) * (d ** -0.5)
    f = lb + (1.0 - lb) * jax.nn.sigmoid(f_ref[...])
    k = 1.0 - f
    logf = jnp.log(f)
    v = i_ref[...]
    tri = _tril(c).astype(F32)
    causal = _tril(c)

    units = [(h, n) for n in range(nc) for h in range(nh)]
    block = lambda x, h, n: x[n * c:(n + 1) * c, h * d:(h + 1) * d]
    cum = {u: _dot(tri, block(logf, *u), HI) for u in units}
    qe, kv, intra, tot = {}, {}, {}, {}
    for u in units:
        last = cum[u][c - 1:c]
        qe[u] = (block(q, *u) * jnp.exp(cum[u])).astype(BF16)
        ke = (block(k, *u) * jnp.exp(-cum[u])).astype(BF16)
        kd = (block(k, *u) * jnp.exp(last - cum[u])).astype(BF16)
        vb = block(v, *u).astype(BF16)
        att = jnp.where(causal, _dot_nt(qe[u], ke), 0.0)
        intra[u] = _dot(att.astype(BF16), vb)
        kv[u] = _dot_tn(vb, kd)
        tot[u] = jnp.exp(last)
    st = [st_ref[h] for h in range(nh)]
    outs = {}
    for n in range(nc):
        for h in range(nh):
            outs[h, n] = intra[h, n] + _dot_nt(qe[h, n], st[h].astype(BF16))
        st = [tot[h, n] * st[h] + kv[h, n] for h in range(nh)]
    for h in range(nh):
        lanes = slice(h * d, (h + 1) * d)
        st_ref[h] = st[h]
        o = jnp.concatenate([outs[h, n] for n in range(nc)], axis=0)
        ms = jnp.mean(o * o, axis=-1, keepdims=True)
        o = o * lax.rsqrt(ms + RMS_EPS) * gn_ref[...]
        o_ref[:, lanes] = o * _silu(g_ref[:, lanes])


def hgrn2(proj, lb, gn_w, bsz, seq, *, tb=256):
    nh, d = HGRN_HEADS, HGRN_D
    width = nh * d
    ns = seq // tb
    col = lambda k: (lambda b, s: (b * ns + s, k))
    return pl.pallas_call(
        functools.partial(_hgrn_kernel, tb=tb),
        grid=(bsz, ns),
        in_specs=[pl.BlockSpec((tb, width), col(0)),
                  pl.BlockSpec((tb, width), col(1)),
                  pl.BlockSpec((tb, width), col(2)),
                  pl.BlockSpec((tb, width), col(3)),
                  pl.BlockSpec((1, width), lambda b, s: (0, 0)),
                  pl.BlockSpec((1, d), lambda b, s: (0, 0))],
        out_specs=pl.BlockSpec((tb, width), col(0)),
        out_shape=jax.ShapeDtypeStruct((bsz * seq, width), F32),
        scratch_shapes=[pltpu.VMEM((nh, d, d), F32)],
        compiler_params=_params("parallel", "arbitrary"),
        name="hgrn2",
    )(proj, proj, proj, proj, lb.reshape(1, width).astype(F32), gn_w.reshape(1, d).astype(F32))


def _s5_weights(a_re, a_im, log_dt, b_re, b_im, c_re, c_im, d_skip):
    ln, gb = S5_L, S5_GB
    g, n = a_re.shape
    p = b_re.shape[-1]
    nb = g // gb
    a_re, a_im, b_re, b_im, c_re, c_im = (t.astype(F32) for t in (a_re, a_im, b_re, b_im, c_re, c_im))
    dt = jnp.exp(log_dt.astype(F32))[:, None]
    lam_re, lam_im = a_re * dt, a_im * dt
    def powers(exponents):
        kk = jnp.asarray(exponents, F32)[:, None, None]
        mag = jnp.exp(kk * lam_re)
        return mag * jnp.cos(kk * lam_im), mag * jnp.sin(kk * lam_im)

    pw_re, pw_im = powers(range(ln + 1))
    abar_re, abar_im = pw_re[1], pw_im[1]
    den = a_re * a_re + a_im * a_im
    coef_re = ((abar_re - 1.0) * a_re + abar_im * a_im) / den
    coef_im = (abar_im * a_re - (abar_re - 1.0) * a_im) / den
    bb_re = coef_re[..., None] * b_re - coef_im[..., None] * b_im
    bb_im = coef_re[..., None] * b_im + coef_im[..., None] * b_re
    eye = jnp.eye(gb, dtype=F32)

    rv, iv = powers(range(ln - 1, -1, -1))
    t1_re = rv[..., None] * bb_re - iv[..., None] * bb_im
    t1_im = rv[..., None] * bb_im + iv[..., None] * bb_re

    def expand_b(t1):
        w = jnp.einsum('tbgnp,gh->btgphn', t1.reshape(ln, nb, gb, n, p), eye)
        return w.reshape(nb, ln * gb * p, gb * n)

    w_b = jnp.concatenate([expand_b(t1_re), expand_b(t1_im)], axis=-1)

    qr, qi = pw_re[1:], pw_im[1:]
    t2_re = c_re[None] * qr[:, :, None, :] - c_im[None] * qi[:, :, None, :]
    t2_im = c_re[None] * qi[:, :, None, :] + c_im[None] * qr[:, :, None, :]

    def expand_c(t2):
        w = jnp.einsum('tbgpn,gh->bgnthp', t2.reshape(ln, nb, gb, p, n), eye)
        return w.reshape(nb, gb * n, ln * gb * p)

    w_c = jnp.concatenate([expand_c(t2_re), -expand_c(t2_im)], axis=1)

    cr = c_re[None] * pw_re[:ln, :, None, :] - c_im[None] * pw_im[:ln, :, None, :]
    ci = c_re[None] * pw_im[:ln, :, None, :] + c_im[None] * pw_re[:ln, :, None, :]
    kd = jnp.einsum('dgpn,gnq->dgpq', cr, bb_re) - jnp.einsum('dgpn,gnq->dgpq', ci, bb_im)
    tau = jnp.arange(ln)
    delta = tau[None, :] - tau[:, None]
    kt = jnp.where((delta >= 0)[:, :, None, None, None], kd[jnp.clip(delta, 0)], 0.0)
    w_k = jnp.einsum('stbgpq,gh->bsgqthp', kt.reshape(ln, ln, nb, gb, p, p), eye)
    w_k = w_k.reshape(nb, ln * gb * p, ln * gb * p)

    al = jnp.stack([pw_re[ln].reshape(nb, 1, gb * n), pw_im[ln].reshape(nb, 1, gb * n)], axis=1)
    dd = jnp.tile(d_skip.astype(F32).reshape(nb, 1, gb * p), (1, 1, ln))
    return w_b.astype(BF16), w_k.astype(BF16), w_c.astype(BF16), al.reshape(nb, 2, gb * n), dd


def _s5_kernel(u_ref, wb_ref, wk_ref, wc_ref, al_ref, dd_ref, o_ref, s_ref, xp_ref, st_ref, *, cb):
    ln = S5_L
    half = st_ref.shape[1] // 2

    @pl.when(pl.program_id(2) == 0)
    def _():
        st_ref[...] = jnp.zeros_like(st_ref)

    u = jnp.concatenate([u_ref[pl.ds(tau, cb, stride=ln), :] for tau in range(ln)], axis=1)
    ub = u.astype(BF16)
    s_ref[...] = _dot(ub, wb_ref[...])
    ar = al_ref[0:1, :]
    ai = al_ref[1:2, :]

    def step(c, carry):
        xr, xi = carry
        xp_ref[pl.ds(c, 1), :] = jnp.concatenate([xr, xi], axis=1)
        srow = s_ref[pl.ds(c, 1), :]
        nr = ar * xr - ai * xi + srow[:, :half]
        ni = ar * xi + ai * xr + srow[:, half:]
        return nr, ni

    x0 = st_ref[...]
    xr, xi = lax.fori_loop(0, cb, step, (x0[:, :half], x0[:, half:]), unroll=8)
    st_ref[...] = jnp.concatenate([xr, xi], axis=1)
    y = _gelu(_dot(ub, wk_ref[...]) + _dot(xp_ref[...].astype(BF16), wc_ref[...]) + dd_ref[...] * u)
    for tau in range(ln):
        o_ref[pl.ds(tau, cb, stride=ln), :] = y[:, tau * LANES:(tau + 1) * LANES]


def s5(proj, col0, weights, bsz, seq, *, cb=256):
    ln = S5_L
    w_b, w_k, w_c, al, dd = weights
    nb = w_b.shape[0]
    t, width = proj.shape
    ncb = seq // ln // cb
    assert width % LANES == 0 and col0 % LANES == 0 and ncb * cb * ln == seq
    kw, ks = w_b.shape[1], w_b.shape[2]

    return pl.pallas_call(
        functools.partial(_s5_kernel, cb=cb),
        grid=(nb, bsz, ncb),
        in_specs=[
            pl.BlockSpec((cb * ln, LANES), lambda g, b, c: (b * ncb + c, col0 // LANES + g)),
            pl.BlockSpec((None, kw, ks), lambda g, b, c: (g, 0, 0)),
            pl.BlockSpec((None, kw, kw), lambda g, b, c: (g, 0, 0)),
            pl.BlockSpec((None, ks, kw), lambda g, b, c: (g, 0, 0)),
            pl.BlockSpec((None, 2, ks // 2), lambda g, b, c: (g, 0, 0)),
            pl.BlockSpec((None, 1, kw), lambda g, b, c: (g, 0, 0))],
        out_specs=pl.BlockSpec((None, cb * ln, LANES), lambda g, b, c: (g, b * ncb + c, 0)),
        out_shape=jax.ShapeDtypeStruct((nb, t, LANES), F32),
        scratch_shapes=[pltpu.VMEM((cb, ks), F32), pltpu.VMEM((cb, ks), F32), pltpu.VMEM((1, ks), F32)],
        compiler_params=_params("parallel", "parallel", "arbitrary"),
        name="s5",
    )(proj, w_b, w_k, w_c, al, dd)


def _even_out_kernel(h_ref, oa_ref, y_ref, glu_ref, wa_ref, wb_ref, o_ref):
    y = jnp.concatenate([y_ref[i] for i in range(y_ref.shape[0])], axis=1)
    ob = y * jax.nn.sigmoid(_dot(y.astype(BF16), glu_ref[...]))
    o_ref[...] = h_ref[...] + _dot(oa_ref[...].astype(BF16), wa_ref[...]) + _dot(ob.astype(BF16), wb_ref[...])


def even_out(h, oa, y4, glu_w, w_a, w_b, *, tm=512):
    t, d = h.shape
    wa_rows, wb_rows = w_a.shape[0], w_b.shape[0]
    nb = y4.shape[0]
    return pl.pallas_call(
        _even_out_kernel,
        grid=(t // tm,),
        in_specs=[pl.BlockSpec((tm, d), lambda i: (i, 0)),
                  pl.BlockSpec((tm, wa_rows), lambda i: (i, 0)),
                  pl.BlockSpec((nb, tm, LANES), lambda i: (0, i, 0)),
                  pl.BlockSpec((wb_rows, wb_rows), lambda i: (0, 0)),
                  pl.BlockSpec((wa_rows, d), lambda i: (0, 0)),
                  pl.BlockSpec((wb_rows, d), lambda i: (0, 0))],
        out_specs=pl.BlockSpec((tm, d), lambda i: (i, 0)),
        out_shape=jax.ShapeDtypeStruct((t, d), F32),
        compiler_params=_params("parallel"),
        name="even_out",
    )(h, oa, y4, glu_w, w_a, w_b)


def _matmul_res_kernel(h_ref, a_ref, w_ref, o_ref):
    o_ref[...] = h_ref[...] + _dot(a_ref[...].astype(BF16), w_ref[...])


def matmul_residual(h, a, w, *, tm=512):
    t, d = h.shape
    k = a.shape[1]
    return pl.pallas_call(
        _matmul_res_kernel,
        grid=(t // tm,),
        in_specs=[pl.BlockSpec((tm, d), lambda i: (i, 0)),
                  pl.BlockSpec((tm, k), lambda i: (i, 0)),
                  pl.BlockSpec((k, d), lambda i: (0, 0))],
        out_specs=pl.BlockSpec((tm, d), lambda i: (i, 0)),
        out_shape=jax.ShapeDtypeStruct((t, d), F32),
        compiler_params=_params("parallel"),
        name="matmul_residual",
    )(h, a, w)


def _split_bf16(x):
    hi = x.astype(BF16)
    return hi, (x - hi.astype(F32)).astype(BF16)


def _dot_split(xs, ys):
    (xh, xl), (yh, yl) = xs, ys
    return _dot(xh, yh) + (_dot(xh, yl) + _dot(xl, yh))


def _gdn_kernel(q_ref, k_ref, v_ref, z_ref, ba_ref, cw_ref, nega_ref, dtb_ref, gn_ref, o_ref,
                st_ref, cb_ref, *, tb):
    c, d, nh = GDN_CHUNK, GDN_D, GDN_HEADS
    width = nh * d
    hist = 8

    @pl.when(pl.program_id(1) == 0)
    def _():
        st_ref[...] = jnp.zeros_like(st_ref)
        cb_ref[:, 0:hist, :] = jnp.zeros((3, hist, width), F32)

    cb_ref[0, hist:hist + tb, :] = q_ref[...]
    cb_ref[1, hist:hist + tb, :] = k_ref[...]
    cb_ref[2, hist:hist + tb, :] = v_ref[...]

    def conv(idx, lanes):
        base = hist - (GDN_CONV - 1)
        col = slice(idx * width + lanes.start, idx * width + lanes.stop)
        acc = cw_ref[0:1, col] * cb_ref[idx, base:base + tb, lanes]
        for j in range(1, GDN_CONV):
            acc = acc + cw_ref[j:j + 1, col] * cb_ref[idx, base + j:base + j + tb, lanes]
        return _silu(acc)

    ba = ba_ref[...]
    beta_all = jax.nn.sigmoid(ba)
    g_all = nega_ref[...] * _softplus(ba + dtb_ref[...])

    incl = _tril(c)
    strict = _tril(c, strict=True)
    tri = incl.astype(F32)
    eye = (lax.broadcasted_iota(jnp.int32, (c, c), 0) == lax.broadcasted_iota(jnp.int32, (c, c), 1)).astype(F32)
    decay = [_dot(tri, g_all[n * c:(n + 1) * c], HI) for n in range(tb // c)]
    decay_t = [dc.T for dc in decay]

    nc = tb // c
    units = [(h, n) for n in range(nc) for h in range(nh)]
    prep = {}
    for h in range(nh):
        lanes = slice(h * d, (h + 1) * d)
        qc, kc, vc = conv(0, lanes), conv(1, lanes), conv(2, lanes)
        q = qc * lax.rsqrt(jnp.sum(qc * qc, axis=-1, keepdims=True) + L2_EPS) * (d ** -0.5)
        k = kc * lax.rsqrt(jnp.sum(kc * kc, axis=-1, keepdims=True) + L2_EPS)
        for n in range(nc):
            sl = slice(n * c, (n + 1) * c)
            dcol = jnp.broadcast_to(decay[n][:, nh + h:nh + h + 1], (c, d))
            diff = dcol[:, :c] - decay_t[n][nh + h:nh + h + 1, :]
            lmask = jnp.where(incl, jnp.exp(jnp.where(incl, diff, 0.0)), 0.0)
            bc = jnp.broadcast_to(beta_all[sl, h:h + 1], (c, d))
            prep[h, n] = dict(dcol=dcol, lmask=lmask, bc=bc, k=k[sl], q=q[sl], v=vc[sl])
    cb_ref[:, 0:hist, :] = cb_ref[:, tb:tb + hist, :]

    xs, t_inv = {}, {}
    for u in units:
        p = prep[u]
        p['kb'] = p['k'] * p['bc']
        p['knb'] = p['k'].astype(BF16)
        a_mat = jnp.where(strict, _dot_nt(p['kb'].astype(BF16), p['knb']) * p['lmask'], 0.0)
        t_inv[u] = eye - a_mat
        xs[u] = _split_bf16(-a_mat)
    for _ in range(int(math.log2(c)) - 1):
        for u in units:
            xs[u] = _split_bf16(_dot_split(xs[u], xs[u]))
        for u in units:
            t_inv[u] = t_inv[u] + _dot_split(_split_bf16(t_inv[u]), xs[u])
    for u in units:
        p = prep[u]
        tib = t_inv[u].astype(BF16)
        edc = jnp.exp(p['dcol'])
        last = p['dcol'][c - 1:c]
        p['value'] = _dot(tib, (p['v'] * p['bc']).astype(BF16))
        p['kcum'] = _dot(tib, (p['kb'] * edc).astype(BF16)).astype(BF16)
        p['attn'] = (_dot_nt(p['q'].astype(BF16), p['knb']) * p['lmask']).astype(BF16)
        p['qdec'] = (p['q'] * edc).astype(BF16)
        p['ktail'] = (p['k'] * jnp.exp(last - p['dcol'])).astype(BF16)
        p['tot'] = jnp.exp(last)

    st = [st_ref[h] for h in range(nh)]
    outs = {}
    for n in range(nc):
        sb = [s.astype(BF16) for s in st]
        vnb = [(prep[h, n]['value'] - _dot(prep[h, n]['kcum'], sb[h])).astype(BF16) for h in range(nh)]
        for h in range(nh):
            p = prep[h, n]
            outs[h, n] = _dot(p['qdec'], sb[h]) + _dot(p['attn'], vnb[h])
        st = [prep[h, n]['tot'] * st[h] + _dot_tn(prep[h, n]['ktail'], vnb[h]) for h in range(nh)]
    for h in range(nh):
        lanes = slice(h * d, (h + 1) * d)
        st_ref[h] = st[h]
        o = jnp.concatenate([outs[h, n] for n in range(nc)], axis=0)
        ms = jnp.mean(o * o, axis=-1, keepdims=True)
        o = o * lax.rsqrt(ms + RMS_EPS) * gn_ref[...]
        o_ref[:, lanes] = o * _silu(z_ref[:, lanes])


def gdn(proj, conv_w, a_log, dt_bias, gn_w, bsz, seq, *, tb=128):
    nh, d = GDN_HEADS, GDN_D
    width = nh * d
    ns = seq // tb
    row = lambda b, s: b * ns + s
    lane = jnp.arange(LANES)
    in_a = (lane >= nh) & (lane < 2 * nh)
    idx = jnp.clip(lane - nh, 0, nh - 1)
    nega = jnp.where(in_a, -jnp.exp(a_log.astype(F32))[idx], 0.0).reshape(1, LANES)
    dtb = jnp.where(in_a, dt_bias.astype(F32)[idx], 0.0).reshape(1, LANES)
    const = lambda b, s: (0, 0)
    return pl.pallas_call(
        functools.partial(_gdn_kernel, tb=tb),
        grid=(bsz, ns),
        in_specs=[pl.BlockSpec((tb, width), lambda b, s: (row(b, s), 0)),
                  pl.BlockSpec((tb, width), lambda b, s: (row(b, s), 1)),
                  pl.BlockSpec((tb, width), lambda b, s: (row(b, s), 2)),
                  pl.BlockSpec((tb, width), lambda b, s: (row(b, s), 3)),
                  pl.BlockSpec((tb, LANES), lambda b, s: (row(b, s), 4 * nh)),
                  pl.BlockSpec((GDN_CONV, 3 * width), const),
                  pl.BlockSpec((1, LANES), const),
                  pl.BlockSpec((1, LANES), const),
                  pl.BlockSpec((1, d), const)],
        out_specs=pl.BlockSpec((tb, width), lambda b, s: (row(b, s), 0)),
        out_shape=jax.ShapeDtypeStruct((bsz * seq, width), F32),
        scratch_shapes=[pltpu.VMEM((nh, d, d), F32), pltpu.VMEM((3, tb + 8, width), F32)],
        compiler_params=_params("parallel", "arbitrary"),
        name="gdn",
    )(proj, proj, proj, proj, proj, conv_w.astype(F32), nega, dtb, gn_w.reshape(1, d).astype(F32))


NO_RANK = 100.0
SUBLANES = 8


def _pack_pair(lo, hi):
    lo_bits = pltpu.bitcast(lo.astype(BF16).astype(F32), jnp.uint32)
    hi_bits = pltpu.bitcast(hi.astype(BF16).astype(F32), jnp.uint32)
    return hi_bits | (lo_bits >> 16)


def _pack_rows(x):
    sub = SUBLANES
    return jnp.concatenate([_pack_pair(x[2 * m * sub:(2 * m + 1) * sub], x[(2 * m + 1) * sub:(2 * m + 2) * sub])
                            for m in range(x.shape[0] // (2 * sub))], axis=0)


def _merge_exchange(n):
    t = n.bit_length() - 1
    pairs, p = [], 1 << (t - 1)
    while p > 0:
        q, r, d = 1 << (t - 1), 0, p
        while d > 0:
            pairs += [(i, i + d) for i in range(n - d) if i & p == r]
            d, q, r = q - p, q // 2, p
        p //= 2
    return pairs


def _sorted_top(s, count):
    sub = SUBLANES
    rows = [s[g * sub:(g + 1) * sub] for g in range(s.shape[0] // sub)]
    for i, j in _merge_exchange(len(rows)):
        rows[i], rows[j] = jnp.maximum(rows[i], rows[j]), jnp.minimum(rows[i], rows[j])
    vals = []
    for r in range(count):
        m = jnp.max(rows[0], axis=0, keepdims=True)
        vals.append(m)
        hit = rows[0] == m
        for g in range(count - 1 - r):
            rows[g] = jnp.where(hit, rows[g + 1], rows[g])
    return vals


def _route_kernel(q_ref, keys_ref, cnt_ref, rank_ref, p_ref, qq_ref):
    kk = PEER_TOPK
    sub = SUBLANES
    for h in range(PEER_HEADS):
        sc, top = [], []
        for c in range(2):
            lo = (2 * h + c) * PEER_DHALF
            qh = q_ref[:, lo:lo + PEER_DHALF].astype(BF16)
            s = _dot_nt(keys_ref[c], qh)
            sc.append(s)
            top.append(_sorted_top(s, kk))
        v1 = jnp.concatenate(top[0], axis=0)
        v2 = jnp.concatenate(top[1], axis=0)
        cands = [top[0][0] + v2[:sub], top[0][0] + v2[sub:]]
        cands += [top[0][a] + v2[:sub] for a in range(1, sub)]
        cands += [v1[sub:] + top[1][0]]
        work = list(cands)
        best = []
        for _ in range(kk + 1):
            m = jnp.max(functools.reduce(jnp.maximum, work), axis=0, keepdims=True)
            best.append(m)
            work = [jnp.where(w == m, -MASKED, w) for w in work]
        theta = 0.5 * (best[kk - 1] + best[kk])
        cmax = best[0]
        taken = [cd >= theta for cd in cands]
        z = functools.reduce(jnp.add, [jnp.sum(jnp.where(tk, jnp.exp(cd - cmax), 0.0), axis=0, keepdims=True)
                                       for tk, cd in zip(taken, cands)])
        ones = [jnp.where(tk, 1.0, 0.0) for tk in taken]
        per_row = [jnp.sum(o, axis=0, keepdims=True) for o in ones[:sub + 1]]
        cnt_rows = [per_row[0] + per_row[1]] + per_row[2:] + [ones[sub + 1][r:r + 1] for r in range(kk - sub)]
        cnt = jnp.zeros_like(sc[0])
        rank = jnp.full(sc[1].shape, NO_RANK, F32)
        for a in range(kk):
            cnt = jnp.where(sc[0] == top[0][a], cnt_rows[a], cnt)
            rank = jnp.where(sc[1] == top[1][a], float(a + 1), rank)
        pp = jnp.where(sc[0] >= top[0][kk - 1], jnp.exp(sc[0] - top[0][0]), 0.0) / z
        qq = jnp.where(sc[1] >= top[1][kk - 1], jnp.exp(sc[1] - top[1][0]), 0.0)
        cnt_ref[h] = _pack_pair(cnt, cnt)
        p_ref[h] = _pack_pair(pp, pp)
        rank_ref[h] = _pack_rows(rank)
        qq_ref[h] = _pack_rows(qq)


def peer_route(q, keys, *, tbk=256):
    t = q.shape[0]
    nk = PEER_NKEYS
    first = pl.BlockSpec((PEER_HEADS, nk, tbk), lambda i: (0, 0, i))
    second = pl.BlockSpec((PEER_HEADS, nk // 2, tbk), lambda i: (0, 0, i))
    shape = lambda rows: jax.ShapeDtypeStruct((PEER_HEADS, rows, t), jnp.uint32)
    return pl.pallas_call(
        _route_kernel,
        grid=(t // tbk,),
        in_specs=[pl.BlockSpec((tbk, q.shape[1]), lambda i: (i, 0)),
                  pl.BlockSpec((2, nk, PEER_DHALF), lambda i: (0, 0, 0))],
        out_specs=[first, second, first, second],
        out_shape=[shape(nk), shape(nk // 2), shape(nk), shape(nk // 2)],
        compiler_params=_params("parallel"),
        name="peer_route",
    )(q, keys)


def _peer_ffn_kernel(x_ref, u_ref, vt_ref, cnt_ref, rank_ref, p_ref, qq_ref, h_ref, nw_ref, o_ref,
                     *scratch, final_norm, tsub):
    nsub = len(scratch) // 3
    ht_refs, wt_refs, acc_refs = scratch[:nsub], scratch[nsub:2 * nsub], scratch[2 * nsub:]
    eb = ht_refs[0].shape[0]
    nk = PEER_NKEYS
    sub = SUBLANES
    j = pl.program_id(1)

    @pl.when(j == 0)
    def _():
        for acc_ref in acc_refs:
            acc_ref[...] = jnp.zeros_like(acc_ref)

    def pairs(words):
        return pltpu.bitcast(words, BF16)

    def activations(s):
        ht_refs[s][...] = _dot_nt(u_ref[...], x_ref[s * tsub:(s + 1) * tsub, :])

    def weigh(s):
        ht_ref, wt_ref = ht_refs[s], wt_refs[s]
        for tl in range(tsub // LANES):
            lanes = slice(s * tsub + tl * LANES, s * tsub + (tl + 1) * LANES)
            local = slice(tl * LANES, (tl + 1) * LANES)
            for il in range(eb // nk):
                cnt = [pairs(jnp.broadcast_to(cnt_ref[hd, il:il + 1, lanes], (sub, LANES)))
                       for hd in range(PEER_HEADS)]
                pr = [pairs(jnp.broadcast_to(p_ref[hd, il:il + 1, lanes], (sub, LANES)))
                      for hd in range(PEER_HEADS)]
                for m in range(nk // (2 * sub)):
                    krows = slice(m * sub, (m + 1) * sub)
                    terms = []
                    for hd in range(PEER_HEADS):
                        sel = pairs(rank_ref[hd, krows, lanes]) <= cnt[hd]
                        val = pr[hd] * pairs(qq_ref[hd, krows, lanes])
                        terms.append(jnp.where(sel, val, jnp.zeros_like(val)))
                    while len(terms) > 1:
                        terms = [a + b for a, b in zip(terms[::2], terms[1::2])]
                    r0 = il * nk + 2 * m * sub
                    act = pairs(_pack_pair(_gelu(ht_ref[r0:r0 + sub, local]),
                                           _gelu(ht_ref[r0 + sub:r0 + 2 * sub, local])))
                    w0 = il * (nk // 2) + m * sub
                    wt_ref[w0:w0 + sub, local] = pltpu.bitcast(terms[0] * act, jnp.uint32)

    def project(s):
        acc_refs[s][...] += _dot(vt_ref[...], pltpu.bitcast(wt_refs[s][...], BF16))

    activations(0)
    for s in range(nsub):
        if s + 1 < nsub:
            activations(s + 1)
        weigh(s)
        project(s)

    @pl.when(j == pl.num_programs(1) - 1)
    def _():
        res = h_ref[...] + jnp.concatenate([acc_ref[...] for acc_ref in acc_refs], axis=1).T
        if final_norm:
            ms = jnp.mean(res * res, axis=-1, keepdims=True)
            res = res * lax.rsqrt(ms + RMS_EPS) * nw_ref[...]
        o_ref[...] = res


def peer_ffn(xn, u, vt, route, h, norm_w, *, final_norm, tb=1024, eb=1024, tsub=256):
    t, d = h.shape
    ne = u.shape[0]
    nk = PEER_NKEYS
    cnt, rank, pp, qq = route
    second = pl.BlockSpec((PEER_HEADS, nk // 2, tb), lambda i, j: (0, 0, i))
    first = pl.BlockSpec((PEER_HEADS, eb // nk, tb), lambda i, j: (0, j, i))
    return pl.pallas_call(
        functools.partial(_peer_ffn_kernel, final_norm=final_norm, tsub=tsub),
        grid=(t // tb, ne // eb),
        in_specs=[pl.BlockSpec((tb, d), lambda i, j: (i, 0)),
                  pl.BlockSpec((eb, d), lambda i, j: (j, 0)),
                  pl.BlockSpec((d, eb), lambda i, j: (0, j)),
                  first, second, first, second,
                  pl.BlockSpec((tb, d), lambda i, j: (i, 0)),
                  pl.BlockSpec((1, d), lambda i, j: (0, 0))],
        out_specs=pl.BlockSpec((tb, d), lambda i, j: (i, 0)),
        out_shape=jax.ShapeDtypeStruct((t, d), F32),
        scratch_shapes=([pltpu.VMEM((eb, tsub), F32)] * (tb // tsub)
                        + [pltpu.VMEM((eb // 2, tsub), jnp.uint32)] * (tb // tsub)
                        + [pltpu.VMEM((d, tsub), F32)] * (tb // tsub)),
        compiler_params=_params("parallel", "arbitrary"),
        name="peer_ffn",
    )(xn, u, vt, cnt, rank, pp, qq, h, norm_w.reshape(1, d).astype(F32))


def _pair_order(v_tab):
    ne, d = v_tab.shape
    sub = SUBLANES
    return v_tab.reshape(ne // (2 * sub), 2, sub, d).transpose(0, 2, 1, 3).reshape(ne, d)


def peer_layer(h, norm_w, w_q, sub_keys, u_tab, v_tab, out_norm_w, *, final_norm):
    q, xn = norm_matmul(h, norm_w, w_q.astype(BF16), emit_xn=True)
    route = peer_route(q, sub_keys.astype(BF16))
    vt = _pair_order(v_tab.astype(BF16)).T
    return peer_ffn(xn, u_tab.astype(BF16), vt, route, h, out_norm_w, final_norm=final_norm)


def kernel(x, norm_mix_w, norm_ffn_w, norm_out_w, ev_in_w, ev_out_w, hgrn_lb, hgrn_gn_w, s5_a_re, s5_a_im, s5_log_dt, s5_b_re, s5_b_im, s5_c_re, s5_c_im, s5_d, s5_glu_w, od_in_w, od_out_w, gdn_conv_w, gdn_a_log, gdn_dt_bias, gdn_gn_w, peer_wq, peer_sub_keys, peer_u, peer_v):
    bsz, seq, d = x.shape
    depth = norm_mix_w.shape[0]
    t = bsz * seq
    h = x.reshape(t, d).astype(F32)
    lb_all = jnp.cumsum(jax.nn.softmax(hgrn_lb.astype(F32), axis=0), axis=0)
    hgrn_w = HGRN_HEADS * HGRN_D
    for layer in range(depth):
        j = layer // 2
        if layer % 2 == 0:
            proj = norm_matmul(h, norm_mix_w[layer], ev_in_w[j].astype(BF16))
            o_a = hgrn2(proj, lb_all[j], hgrn_gn_w[j], bsz, seq)
            weights = _s5_weights(s5_a_re[j], s5_a_im[j], s5_log_dt[j], s5_b_re[j], s5_b_im[j],
                                  s5_c_re[j], s5_c_im[j], s5_d[j])
            y4 = s5(proj, 4 * hgrn_w, weights, bsz, seq)
            w_out = ev_out_w[j].astype(BF16)
            h = even_out(h, o_a, y4, s5_glu_w[j].astype(BF16), w_out[:hgrn_w], w_out[hgrn_w:])
        else:
            w_in = od_in_w[j]
            pad = (-w_in.shape[1]) % LANES
            w_in = jnp.pad(w_in, ((0, 0), (0, pad))).astype(BF16)
            proj = norm_matmul(h, norm_mix_w[layer], w_in)
            o_c = gdn(proj, gdn_conv_w[j], gdn_a_log[j], gdn_dt_bias[j], gdn_gn_w[j], bsz, seq)
            h = matmul_residual(h, o_c, od_out_w[j].astype(BF16))
        h = peer_layer(h, norm_ffn_w[layer], peer_wq[layer], peer_sub_keys[layer], peer_u[layer],
                       peer_v[layer], norm_out_w, final_norm=(layer == depth - 1))
    return h.reshape(bsz, seq, d)
```

```python
import functools
import math

import jax
import jax.numpy as jnp
from jax import lax
from jax.experimental import pallas as pl
from jax.experimental.pallas import tpu as pltpu

F32 = jnp.float32
BF16 = jnp.bfloat16
HI = lax.Precision.HIGHEST

RMS_EPS = 1e-6
L2_EPS = 1e-6
LANES = 128
VMEM_LIMIT = 56 * 1024 * 1024

HGRN_HEADS, HGRN_D, HGRN_CHUNK = 4, 128, 32
S5_GROUP, S5_STATE, S5_L = 16, 64, 8
S5_GB = LANES // S5_GROUP
GDN_HEADS, GDN_D, GDN_CONV, GDN_CHUNK = 8, 128, 4, 64
PEER_HEADS, PEER_NKEYS, PEER_DHALF, PEER_TOPK = 8, 128, 128, 16
MASKED = 1e30


def _dot(a, b, precision=None):
    return jnp.dot(a, b, preferred_element_type=F32, precision=precision)


def _dot_nt(a, b):
    return lax.dot_general(a, b, (((1,), (1,)), ((), ())), preferred_element_type=F32)


def _dot_tn(a, b):
    return lax.dot_general(a, b, (((0,), (0,)), ((), ())), preferred_element_type=F32)


def _silu(x):
    return x * jax.nn.sigmoid(x)


def _gelu(x):
    return 0.5 * x * (1.0 + lax.erf(x * (2.0 ** -0.5)))


def _softplus(x):
    return jnp.maximum(x, 0.0) + jnp.log1p(jnp.exp(-jnp.abs(x)))


def _params(*sem):
    return pltpu.CompilerParams(dimension_semantics=sem, vmem_limit_bytes=VMEM_LIMIT)


def _tril(n, strict=False):
    r = lax.broadcasted_iota(jnp.int32, (n, n), 0)
    c = lax.broadcasted_iota(jnp.int32, (n, n), 1)
    return (r > c) if strict else (r >= c)


def _norm_matmul_kernel(emit_xn, x_ref, nw_ref, w_ref, o_ref, *rest):
    x = x_ref[...]
    ms = jnp.mean(x * x, axis=-1, keepdims=True)
    xn = (x * lax.rsqrt(ms + RMS_EPS) * nw_ref[...]).astype(BF16)
    if emit_xn:
        rest[0][...] = xn
    o_ref[...] = _dot(xn, w_ref[...])


def norm_matmul(h, nw, w, *, emit_xn=False, tm=512):
    t, d = h.shape
    n = w.shape[1]
    assert t % tm == 0 and n % LANES == 0
    out_shape = [jax.ShapeDtypeStruct((t, n), F32)]
    out_specs = [pl.BlockSpec((tm, n), lambda i: (i, 0))]
    if emit_xn:
        out_shape.append(jax.ShapeDtypeStruct((t, d), BF16))
        out_specs.append(pl.BlockSpec((tm, d), lambda i: (i, 0)))
    res = pl.pallas_call(
        functools.partial(_norm_matmul_kernel, emit_xn),
        grid=(t // tm,),
        in_specs=[pl.BlockSpec((tm, d), lambda i: (i, 0)),
                  pl.BlockSpec((1, d), lambda i: (0, 0)),
                  pl.BlockSpec((d, n), lambda i: (0, 0))],
        out_specs=out_specs,
        out_shape=out_shape,
        compiler_params=_params("parallel"),
        name="norm_matmul",
    )(h, nw.reshape(1, d).astype(F32), w)
    return res if emit_xn else res[0]


def _hgrn_kernel(q_ref, f_ref, i_ref, g_ref, lb_ref, gn_ref, o_ref, st_ref, *, tb):
    c, d, nh = HGRN_CHUNK, HGRN_D, HGRN_HEADS
    nc = tb // c

    @pl.when(pl.program_id(1) == 0)
    def _():
        st_ref[...] = jnp.zeros_like(st_ref)

    lb = lb_ref[...]
    q = _silu(q_ref[...]) * (d ** -0.5)
    f = lb + (1.0 - lb) * jax.nn.sigmoid(f_ref[...])
    k = 1.0 - f
    logf = jnp.log(f)
    v = i_ref[...]
    tri = _tril(c).astype(F32)
    causal = _tril(c)

    units = [(h, n) for n in range(nc) for h in range(nh)]
    block = lambda x, h, n: x[n * c:(n + 1) * c, h * d:(h + 1) * d]
    cum = {u: _dot(tri, block(logf, *u), HI) for u in units}
    qe, kv, intra, tot = {}, {}, {}, {}
    for u in units:
        last = cum[u][c - 1:c]
        qe[u] = (block(q, *u) * jnp.exp(cum[u])).astype(BF16)
        ke = (block(k, *u) * jnp.exp(-cum[u])).astype(BF16)
        kd = (block(k, *u) * jnp.exp(last - cum[u])).astype(BF16)
        vb = block(v, *u).astype(BF16)
        att = jnp.where(causal, _dot_nt(qe[u], ke), 0.0)
        intra[u] = _dot(att.astype(BF16), vb)
        kv[u] = _dot_tn(vb, kd)
        tot[u] = jnp.exp(last)
    st = [st_ref[h] for h in range(nh)]
    outs = {}
    for n in range(nc):
        for h in range(nh):
            outs[h, n] = intra[h, n] + _dot_nt(qe[h, n], st[h].astype(BF16))
        st = [tot[h, n] * st[h] + kv[h, n] for h in range(nh)]
    for h in range(nh):
        lanes = slice(h * d, (h + 1) * d)
        st_ref[h] = st[h]
        o = jnp.concatenate([outs[h, n] for n in range(nc)], axis=0)
        ms = jnp.mean(o * o, axis=-1, keepdims=True)
        o = o * lax.rsqrt(ms + RMS_EPS) * gn_ref[...]
        o_ref[:, lanes] = o * _silu(g_ref[:, lanes])


def hgrn2(proj, lb, gn_w, bsz, seq, *, tb=256):
    nh, d = HGRN_HEADS, HGRN_D
    width = nh * d
    ns = seq // tb
    col = lambda k: (lambda b, s: (b * ns + s, k))
    return pl.pallas_call(
        functools.partial(_hgrn_kernel, tb=tb),
        grid=(bsz, ns),
        in_specs=[pl.BlockSpec((tb, width), col(0)),
                  pl.BlockSpec((tb, width), col(1)),
                  pl.BlockSpec((tb, width), col(2)),
                  pl.BlockSpec((tb, width), col(3)),
                  pl.BlockSpec((1, width), lambda b, s: (0, 0)),
                  pl.BlockSpec((1, d), lambda b, s: (0, 0))],
        out_specs=pl.BlockSpec((tb, width), col(0)),
        out_shape=jax.ShapeDtypeStruct((bsz * seq, width), F32),
        scratch_shapes=[pltpu.VMEM((nh, d, d), F32)],
        compiler_params=_params("parallel", "arbitrary"),
        name="hgrn2",
    )(proj, proj, proj, proj, lb.reshape(1, width).astype(F32), gn_w.reshape(1, d).astype(F32))


def _s5_weights(a_re, a_im, log_dt, b_re, b_im, c_re, c_im, d_skip):
    ln, gb = S5_L, S5_GB
    g, n = a_re.shape
    p = b_re.shape[-1]
    nb = g // gb
    a_re, a_im, b_re, b_im, c_re, c_im = (t.astype(F32) for t in (a_re, a_im, b_re, b_im, c_re, c_im))
    dt = jnp.exp(log_dt.astype(F32))[:, None]
    lam_re, lam_im = a_re * dt, a_im * dt
    def powers(exponents):
        kk = jnp.asarray(exponents, F32)[:, None, None]
        mag = jnp.exp(kk * lam_re)
        return mag * jnp.cos(kk * lam_im), mag * jnp.sin(kk * lam_im)

    pw_re, pw_im = powers(range(ln + 1))
    abar_re, abar_im = pw_re[1], pw_im[1]
    den = a_re * a_re + a_im * a_im
    coef_re = ((abar_re - 1.0) * a_re + abar_im * a_im) / den
    coef_im = (abar_im * a_re - (abar_re - 1.0) * a_im) / den
    bb_re = coef_re[..., None] * b_re - coef_im[..., None] * b_im
    bb_im = coef_re[..., None] * b_im + coef_im[..., None] * b_re
    eye = jnp.eye(gb, dtype=F32)

    rv, iv = powers(range(ln - 1, -1, -1))
    t1_re = rv[..., None] * bb_re - iv[..., None] * bb_im
    t1_im = rv[..., None] * bb_im + iv[..., None] * bb_re

    def expand_b(t1):
        w = jnp.einsum('tbgnp,gh->btgphn', t1.reshape(ln, nb, gb, n, p), eye)
        return w.reshape(nb, ln * gb * p, gb * n)

    w_b = jnp.concatenate([expand_b(t1_re), expand_b(t1_im)], axis=-1)

    qr, qi = pw_re[1:], pw_im[1:]
    t2_re = c_re[None] * qr[:, :, None, :] - c_im[None] * qi[:, :, None, :]
    t2_im = c_re[None] * qi[:, :, None, :] + c_im[None] * qr[:, :, None, :]

    def expand_c(t2):
        w = jnp.einsum('tbgpn,gh->bgnthp', t2.reshape(ln, nb, gb, p, n), eye)
        return w.reshape(nb, gb * n, ln * gb * p)

    w_c = jnp.concatenate([expand_c(t2_re), -expand_c(t2_im)], axis=1)

    cr = c_re[None] * pw_re[:ln, :, None, :] - c_im[None] * pw_im[:ln, :, None, :]
    ci = c_re[None] * pw_im[:ln, :, None, :] + c_im[None] * pw_re[:ln, :, None, :]
    kd = jnp.einsum('dgpn,gnq->dgpq', cr, bb_re) - jnp.einsum('dgpn,gnq->dgpq', ci, bb_im)
    tau = jnp.arange(ln)
    delta = tau[None, :] - tau[:, None]
    kt = jnp.where((delta >= 0)[:, :, None, None, None], kd[jnp.clip(delta, 0)], 0.0)
    w_k = jnp.einsum('stbgpq,gh->bsgqthp', kt.reshape(ln, ln, nb, gb, p, p), eye)
    w_k = w_k.reshape(nb, ln * gb * p, ln * gb * p)

    al = jnp.stack([pw_re[ln].reshape(nb, 1, gb * n), pw_im[ln].reshape(nb, 1, gb * n)], axis=1)
    dd = jnp.tile(d_skip.astype(F32).reshape(nb, 1, gb * p), (1, 1, ln))
    return w_b.astype(BF16), w_k.astype(BF16), w_c.astype(BF16), al.reshape(nb, 2, gb * n), dd


def _s5_kernel(u_ref, wb_ref, wk_ref, wc_ref, al_ref, dd_ref, o_ref, s_ref, xp_ref, st_ref, *, cb):
    ln = S5_L
    half = st_ref.shape[1] // 2

    @pl.when(pl.program_id(2) == 0)
    def _():
        st_ref[...] = jnp.zeros_like(st_ref)

    u = jnp.concatenate([u_ref[pl.ds(tau, cb, stride=ln), :] for tau in range(ln)], axis=1)
    ub = u.astype(BF16)
    s_ref[...] = _dot(ub, wb_ref[...])
    ar = al_ref[0:1, :]
    ai = al_ref[1:2, :]

    def step(c, carry):
        xr, xi = carry
        xp_ref[pl.ds(c, 1), :] = jnp.concatenate([xr, xi], axis=1)
        srow = s_ref[pl.ds(c, 1), :]
        nr = ar * xr - ai * xi + srow[:, :half]
        ni = ar * xi + ai * xr + srow[:, half:]
        return nr, ni

    x0 = st_ref[...]
    xr, xi = lax.fori_loop(0, cb, step, (x0[:, :half], x0[:, half:]), unroll=8)
    st_ref[...] = jnp.concatenate([xr, xi], axis=1)
    y = _gelu(_dot(ub, wk_ref[...]) + _dot(xp_ref[...].astype(BF16), wc_ref[...]) + dd_ref[...] * u)
    for tau in range(ln):
        o_ref[pl.ds(tau, cb, stride=ln), :] = y[:, tau * LANES:(tau + 1) * LANES]


def s5(proj, col0, weights, bsz, seq, *, cb=256):
    ln = S5_L
    w_b, w_k, w_c, al, dd = weights
    nb = w_b.shape[0]
    t, width = proj.shape
    ncb = seq // ln // cb
    assert width % LANES == 0 and col0 % LANES == 0 and ncb * cb * ln == seq
    kw, ks = w_b.shape[1], w_b.shape[2]

    return pl.pallas_call(
        functools.partial(_s5_kernel, cb=cb),
        grid=(nb, bsz, ncb),
        in_specs=[
            pl.BlockSpec((cb * ln, LANES), lambda g, b, c: (b * ncb + c, col0 // LANES + g)),
            pl.BlockSpec((None, kw, ks), lambda g, b, c: (g, 0, 0)),
            pl.BlockSpec((None, kw, kw), lambda g, b, c: (g, 0, 0)),
            pl.BlockSpec((None, ks, kw), lambda g, b, c: (g, 0, 0)),
            pl.BlockSpec((None, 2, ks // 2), lambda g, b, c: (g, 0, 0)),
            pl.BlockSpec((None, 1, kw), lambda g, b, c: (g, 0, 0))],
        out_specs=pl.BlockSpec((None, cb * ln, LANES), lambda g, b, c: (g, b * ncb + c, 0)),
        out_shape=jax.ShapeDtypeStruct((nb, t, LANES), F32),
        scratch_shapes=[pltpu.VMEM((cb, ks), F32), pltpu.VMEM((cb, ks), F32), pltpu.VMEM((1, ks), F32)],
        compiler_params=_params("parallel", "parallel", "arbitrary"),
        name="s5",
    )(proj, w_b, w_k, w_c, al, dd)


def _even_out_kernel(h_ref, oa_ref, y_ref, glu_ref, wa_ref, wb_ref, o_ref):
    y = jnp.concatenate([y_ref[i] for i in range(y_ref.shape[0])], axis=1)
    ob = y * jax.nn.sigmoid(_dot(y.astype(BF16), glu_ref[...]))
    o_ref[...] = h_ref[...] + _dot(oa_ref[...].astype(BF16), wa_ref[...]) + _dot(ob.astype(BF16), wb_ref[...])


def even_out(h, oa, y4, glu_w, w_a, w_b, *, tm=512):
    t, d = h.shape
    wa_rows, wb_rows = w_a.shape[0], w_b.shape[0]
    nb = y4.shape[0]
    return pl.pallas_call(
        _even_out_kernel,
        grid=(t // tm,),
        in_specs=[pl.BlockSpec((tm, d), lambda i: (i, 0)),
                  pl.BlockSpec((tm, wa_rows), lambda i: (i, 0)),
                  pl.BlockSpec((nb, tm, LANES), lambda i: (0, i, 0)),
                  pl.BlockSpec((wb_rows, wb_rows), lambda i: (0, 0)),
                  pl.BlockSpec((wa_rows, d), lambda i: (0, 0)),
                  pl.BlockSpec((wb_rows, d), lambda i: (0, 0))],
        out_specs=pl.BlockSpec((tm, d), lambda i: (i, 0)),
        out_shape=jax.ShapeDtypeStruct((t, d), F32),
        compiler_params=_params("parallel"),
        name="even_out",
    )(h, oa, y4, glu_w, w_a, w_b)


def _matmul_res_kernel(h_ref, a_ref, w_ref, o_ref):
    o_ref[...] = h_ref[...] + _dot(a_ref[...].astype(BF16), w_ref[...])


def matmul_residual(h, a, w, *, tm=512):
    t, d = h.shape
    k = a.shape[1]
    return pl.pallas_call(
        _matmul_res_kernel,
        grid=(t // tm,),
        in_specs=[pl.BlockSpec((tm, d), lambda i: (i, 0)),
                  pl.BlockSpec((tm, k), lambda i: (i, 0)),
                  pl.BlockSpec((k, d), lambda i: (0, 0))],
        out_specs=pl.BlockSpec((tm, d), lambda i: (i, 0)),
        out_shape=jax.ShapeDtypeStruct((t, d), F32),
        compiler_params=_params("parallel"),
        name="matmul_residual",
    )(h, a, w)


def _split_bf16(x):
    hi = x.astype(BF16)
    return hi, (x - hi.astype(F32)).astype(BF16)


def _dot_split(xs, ys):
    (xh, xl), (yh, yl) = xs, ys
    return _dot(xh, yh) + (_dot(xh, yl) + _dot(xl, yh))


def _gdn_kernel(q_ref, k_ref, v_ref, z_ref, ba_ref, cw_ref, nega_ref, dtb_ref, gn_ref, o_ref,
                st_ref, cb_ref, *, tb):
    c, d, nh = GDN_CHUNK, GDN_D, GDN_HEADS
    width = nh * d
    hist = 8

    @pl.when(pl.program_id(1) == 0)
    def _():
        st_ref[...] = jnp.zeros_like(st_ref)
        cb_ref[:, 0:hist, :] = jnp.zeros((3, hist, width), F32)

    cb_ref[0, hist:hist + tb, :] = q_ref[...]
    cb_ref[1, hist:hist + tb, :] = k_ref[...]
    cb_ref[2, hist:hist + tb, :] = v_ref[...]

    def conv(idx, lanes):
        base = hist - (GDN_CONV - 1)
        col = slice(idx * width + lanes.start, idx * width + lanes.stop)
        acc = cw_ref[0:1, col] * cb_ref[idx, base:base + tb, lanes]
        for j in range(1, GDN_CONV):
            acc = acc + cw_ref[j:j + 1, col] * cb_ref[idx, base + j:base + j + tb, lanes]
        return _silu(acc)

    ba = ba_ref[...]
    beta_all = jax.nn.sigmoid(ba)
    g_all = nega_ref[...] * _softplus(ba + dtb_ref[...])

    incl = _tril(c)
    strict = _tril(c, strict=True)
    tri = incl.astype(F32)
    eye = (lax.broadcasted_iota(jnp.int32, (c, c), 0) == lax.broadcasted_iota(jnp.int32, (c, c), 1)).astype(F32)
    decay = [_dot(tri, g_all[n * c:(n + 1) * c], HI) for n in range(tb // c)]
    decay_t = [dc.T for dc in decay]

    nc = tb // c
    units = [(h, n) for n in range(nc) for h in range(nh)]
    prep = {}
    for h in range(nh):
        lanes = slice(h * d, (h + 1) * d)
        qc, kc, vc = conv(0, lanes), conv(1, lanes), conv(2, lanes)
        q = qc * lax.rsqrt(jnp.sum(qc * qc, axis=-1, keepdims=True) + L2_EPS) * (d ** -0.5)
        k = kc * lax.rsqrt(jnp.sum(kc * kc, axis=-1, keepdims=True) + L2_EPS)
        for n in range(nc):
            sl = slice(n * c, (n + 1) * c)
            dcol = jnp.broadcast_to(decay[n][:, nh + h:nh + h + 1], (c, d))
            diff = dcol[:, :c] - decay_t[n][nh + h:nh + h + 1, :]
            lmask = jnp.where(incl, jnp.exp(jnp.where(incl, diff, 0.0)), 0.0)
            bc = jnp.broadcast_to(beta_all[sl, h:h + 1], (c, d))
            prep[h, n] = dict(dcol=dcol, lmask=lmask, bc=bc, k=k[sl], q=q[sl], v=vc[sl])
    cb_ref[:, 0:hist, :] = cb_ref[:, tb:tb + hist, :]

    xs, t_inv = {}, {}
    for u in units:
        p = prep[u]
        p['kb'] = p['k'] * p['bc']
        p['knb'] = p['k'].astype(BF16)
        a_mat = jnp.where(strict, _dot_nt(p['kb'].astype(BF16), p['knb']) * p['lmask'], 0.0)
        t_inv[u] = eye - a_mat
        xs[u] = _split_bf16(-a_mat)
    for _ in range(int(math.log2(c)) - 1):
        for u in units:
            xs[u] = _split_bf16(_dot_split(xs[u], xs[u]))
        for u in units:
            t_inv[u] = t_inv[u] + _dot_split(_split_bf16(t_inv[u]), xs[u])
    for u in units:
        p = prep[u]
        tib = t_inv[u].astype(BF16)
        edc = jnp.exp(p['dcol'])
        last = p['dcol'][c - 1:c]
        p['value'] = _dot(tib, (p['v'] * p['bc']).astype(BF16))
        p['kcum'] = _dot(tib, (p['kb'] * edc).astype(BF16)).astype(BF16)
        p['attn'] = (_dot_nt(p['q'].astype(BF16), p['knb']) * p['lmask']).astype(BF16)
        p['qdec'] = (p['q'] * edc).astype(BF16)
        p['ktail'] = (p['k'] * jnp.exp(last - p['dcol'])).astype(BF16)
        p['tot'] = jnp.exp(last)

    st = [st_ref[h] for h in range(nh)]
    outs = {}
    for n in range(nc):
        sb = [s.astype(BF16) for s in st]
        vnb = [(prep[h, n]['value'] - _dot(prep[h, n]['kcum'], sb[h])).astype(BF16) for h in range(nh)]
        for h in range(nh):
            p = prep[h, n]
            outs[h, n] = _dot(p['qdec'], sb[h]) + _dot(p['attn'], vnb[h])
        st = [prep[h, n]['tot'] * st[h] + _dot_tn(prep[h, n]['ktail'], vnb[h]) for h in range(nh)]
    for h in range(nh):
        lanes = slice(h * d, (h + 1) * d)
        st_ref[h] = st[h]
        o = jnp.concatenate([outs[h, n] for n in range(nc)], axis=0)
        ms = jnp.mean(o * o, axis=-1, keepdims=True)
        o = o * lax.rsqrt(ms + RMS_EPS) * gn_ref[...]
        o_ref[:, lanes] = o * _silu(z_ref[:, lanes])


def gdn(proj, conv_w, a_log, dt_bias, gn_w, bsz, seq, *, tb=128):
    nh, d = GDN_HEADS, GDN_D
    width = nh * d
    ns = seq // tb
    row = lambda b, s: b * ns + s
    lane = jnp.arange(LANES)
    in_a = (lane >= nh) & (lane < 2 * nh)
    idx = jnp.clip(lane - nh, 0, nh - 1)
    nega = jnp.where(in_a, -jnp.exp(a_log.astype(F32))[idx], 0.0).reshape(1, LANES)
    dtb = jnp.where(in_a, dt_bias.astype(F32)[idx], 0.0).reshape(1, LANES)
    const = lambda b, s: (0, 0)
    return pl.pallas_call(
        functools.partial(_gdn_kernel, tb=tb),
        grid=(bsz, ns),
        in_specs=[pl.BlockSpec((tb, width), lambda b, s: (row(b, s), 0)),
                  pl.BlockSpec((tb, width), lambda b, s: (row(b, s), 1)),
                  pl.BlockSpec((tb, width), lambda b, s: (row(b, s), 2)),
                  pl.BlockSpec((tb, width), lambda b, s: (row(b, s), 3)),
                  pl.BlockSpec((tb, LANES), lambda b, s: (row(b, s), 4 * nh)),
                  pl.BlockSpec((GDN_CONV, 3 * width), const),
                  pl.BlockSpec((1, LANES), const),
                  pl.BlockSpec((1, LANES), const),
                  pl.BlockSpec((1, d), const)],
        out_specs=pl.BlockSpec((tb, width), lambda b, s: (row(b, s), 0)),
        out_shape=jax.ShapeDtypeStruct((bsz * seq, width), F32),
        scratch_shapes=[pltpu.VMEM((nh, d, d), F32), pltpu.VMEM((3, tb + 8, width), F32)],
        compiler_params=_params("parallel", "arbitrary"),
        name="gdn",
    )(proj, proj, proj, proj, proj, conv_w.astype(F32), nega, dtb, gn_w.reshape(1, d).astype(F32))


NO_RANK = 100.0
SUBLANES = 8


def _pack_pair(lo, hi):
    lo_bits = pltpu.bitcast(lo.astype(BF16).astype(F32), jnp.uint32)
    hi_bits = pltpu.bitcast(hi.astype(BF16).astype(F32), jnp.uint32)
    return hi_bits | (lo_bits >> 16)


def _pack_rows(x):
    sub = SUBLANES
    return jnp.concatenate([_pack_pair(x[2 * m * sub:(2 * m + 1) * sub], x[(2 * m + 1) * sub:(2 * m + 2) * sub])
                            for m in range(x.shape[0] // (2 * sub))], axis=0)


def _merge_exchange(n):
    t = n.bit_length() - 1
    pairs, p = [], 1 << (t - 1)
    while p > 0:
        q, r, d = 1 << (t - 1), 0, p
        while d > 0:
            pairs += [(i, i + d) for i in range(n - d) if i & p == r]
            d, q, r = q - p, q // 2, p
        p //= 2
    return pairs


def _sorted_top(s, count):
    sub = SUBLANES
    rows = [s[g * sub:(g + 1) * sub] for g in range(s.shape[0] // sub)]
    for i, j in _merge_exchange(len(rows)):
        rows[i], rows[j] = jnp.maximum(rows[i], rows[j]), jnp.minimum(rows[i], rows[j])
    vals = []
    for r in range(count):
        m = jnp.max(rows[0], axis=0, keepdims=True)
        vals.append(m)
        hit = rows[0] == m
        for g in range(count - 1 - r):
            rows[g] = jnp.where(hit, rows[g + 1], rows[g])
    return vals


def _route_kernel(q_ref, keys_ref, cnt_ref, rank_ref, p_ref, qq_ref):
    kk = PEER_TOPK
    sub = SUBLANES
    for h in range(PEER_HEADS):
        sc, top = [], []
        for c in range(2):
            lo = (2 * h + c) * PEER_DHALF
            qh = q_ref[:, lo:lo + PEER_DHALF].astype(BF16)
            s = _dot_nt(keys_ref[c], qh)
            sc.append(s)
            top.append(_sorted_top(s, kk))
        v1 = jnp.concatenate(top[0], axis=0)
        v2 = jnp.concatenate(top[1], axis=0)
        cands = [top[0][0] + v2[:sub], top[0][0] + v2[sub:]]
        cands += [top[0][a] + v2[:sub] for a in range(1, sub)]
        cands += [v1[sub:] + top[1][0]]
        work = list(cands)
        best = []
        for _ in range(kk + 1):
            m = jnp.max(functools.reduce(jnp.maximum, work), axis=0, keepdims=True)
            best.append(m)
            work = [jnp.where(w == m, -MASKED, w) for w in work]
        theta = 0.5 * (best[kk - 1] + best[kk])
        cmax = best[0]
        taken = [cd >= theta for cd in cands]
        z = functools.reduce(jnp.add, [jnp.sum(jnp.where(tk, jnp.exp(cd - cmax), 0.0), axis=0, keepdims=True)
                                       for tk, cd in zip(taken, cands)])
        ones = [jnp.where(tk, 1.0, 0.0) for tk in taken]
        per_row = [jnp.sum(o, axis=0, keepdims=True) for o in ones[:sub + 1]]
        cnt_rows = [per_row[0] + per_row[1]] + per_row[2:] + [ones[sub + 1][r:r + 1] for r in range(kk - sub)]
        cnt = jnp.zeros_like(sc[0])
        rank = jnp.full(sc[1].shape, NO_RANK, F32)
        for a in range(kk):
            cnt = jnp.where(sc[0] == top[0][a], cnt_rows[a], cnt)
            rank = jnp.where(sc[1] == top[1][a], float(a + 1), rank)
        pp = jnp.where(sc[0] >= top[0][kk - 1], jnp.exp(sc[0] - top[0][0]), 0.0) / z
        qq = jnp.where(sc[1] >= top[1][kk - 1], jnp.exp(sc[1] - top[1][0]), 0.0)
        cnt_ref[h] = _pack_pair(cnt, cnt)
        p_ref[h] = _pack_pair(pp, pp)
        rank_ref[h] = _pack_rows(rank)
        qq_ref[h] = _pack_rows(qq)


def peer_route(q, keys, *, tbk=256):
    t = q.shape[0]
    nk = PEER_NKEYS
    first = pl.BlockSpec((PEER_HEADS, nk, tbk), lambda i: (0, 0, i))
    second = pl.BlockSpec((PEER_HEADS, nk // 2, tbk), lambda i: (0, 0, i))
    shape = lambda rows: jax.ShapeDtypeStruct((PEER_HEADS, rows, t), jnp.uint32)
    return pl.pallas_call(
        _route_kernel,
        grid=(t // tbk,),
        in_specs=[pl.BlockSpec((tbk, q.shape[1]), lambda i: (i, 0)),
                  pl.BlockSpec((2, nk, PEER_DHALF), lambda i: (0, 0, 0))],
        out_specs=[first, second, first, second],
        out_shape=[shape(nk), shape(nk // 2), shape(nk), shape(nk // 2)],
        compiler_params=_params("parallel"),
        name="peer_route",
    )(q, keys)


def _peer_ffn_kernel(x_ref, u_ref, un_ref, vt_ref, cnt_ref, rank_ref, p_ref, qq_ref, h_ref, nw_ref, o_ref,
                     *scratch, final_norm, tsub):
    nsub = len(scratch) // 3
    ht_refs, wt_refs, acc_refs = scratch[:nsub], scratch[nsub:2 * nsub], scratch[2 * nsub:]
    eb = ht_refs[0].shape[0]
    nk = PEER_NKEYS
    sub = SUBLANES
    j = pl.program_id(1)

    def pairs(words):
        return pltpu.bitcast(words, BF16)

    def activations(s, experts_ref=u_ref):
        ht_refs[s][...] = _dot_nt(experts_ref[...], x_ref[s * tsub:(s + 1) * tsub, :])

    @pl.when(j == 0)
    def _():
        for acc_ref in acc_refs:
            acc_ref[...] = jnp.zeros_like(acc_ref)
        activations(0)

    def weigh(s):
        ht_ref, wt_ref = ht_refs[s], wt_refs[s]
        for tl in range(tsub // LANES):
            lanes = slice(s * tsub + tl * LANES, s * tsub + (tl + 1) * LANES)
            local = slice(tl * LANES, (tl + 1) * LANES)
            for il in range(eb // nk):
                cnt = [pairs(jnp.broadcast_to(cnt_ref[hd, il:il + 1, lanes], (sub, LANES)))
                       for hd in range(PEER_HEADS)]
                pr = [pairs(jnp.broadcast_to(p_ref[hd, il:il + 1, lanes], (sub, LANES)))
                      for hd in range(PEER_HEADS)]
                for m in range(nk // (2 * sub)):
                    krows = slice(m * sub, (m + 1) * sub)
                    terms = []
                    for hd in range(PEER_HEADS):
                        sel = pairs(rank_ref[hd, krows, lanes]) <= cnt[hd]
                        val = pr[hd] * pairs(qq_ref[hd, krows, lanes])
                        terms.append(jnp.where(sel, val, jnp.zeros_like(val)))
                    while len(terms) > 1:
                        terms = [a + b for a, b in zip(terms[::2], terms[1::2])]
                    r0 = il * nk + 2 * m * sub
                    act = pairs(_pack_pair(_gelu(ht_ref[r0:r0 + sub, local]),
                                           _gelu(ht_ref[r0 + sub:r0 + 2 * sub, local])))
                    w0 = il * (nk // 2) + m * sub
                    wt_ref[w0:w0 + sub, local] = pltpu.bitcast(terms[0] * act, jnp.uint32)

    def project(s):
        acc_refs[s][...] += _dot(vt_ref[...], pltpu.bitcast(wt_refs[s][...], BF16))

    for s in range(nsub):
        if s + 1 < nsub:
            activations(s + 1)
        weigh(s)
        if s == 0:
            activations(0, un_ref)
        project(s)

    @pl.when(j == pl.num_programs(1) - 1)
    def _():
        res = h_ref[...] + jnp.concatenate([acc_ref[...] for acc_ref in acc_refs], axis=1).T
        if final_norm:
            ms = jnp.mean(res * res, axis=-1, keepdims=True)
            res = res * lax.rsqrt(ms + RMS_EPS) * nw_ref[...]
        o_ref[...] = res


def peer_ffn(xn, u, vt, route, h, norm_w, *, final_norm, tb=1024, eb=1024, tsub=256):
    t, d = h.shape
    ne = u.shape[0]
    nk = PEER_NKEYS
    cnt, rank, pp, qq = route
    second = pl.BlockSpec((PEER_HEADS, nk // 2, tb), lambda i, j: (0, 0, i))
    first = pl.BlockSpec((PEER_HEADS, eb // nk, tb), lambda i, j: (0, j, i))
    return pl.pallas_call(
        functools.partial(_peer_ffn_kernel, final_norm=final_norm, tsub=tsub),
        grid=(t // tb, ne // eb),
        in_specs=[pl.BlockSpec((tb, d), lambda i, j: (i, 0)),
                  pl.BlockSpec((eb, d), lambda i, j: (j, 0)),
                  pl.BlockSpec((eb, d), lambda i, j: (jnp.minimum(j + 1, ne // eb - 1), 0)),
                  pl.BlockSpec((d, eb), lambda i, j: (0, j)),
                  first, second, first, second,
                  pl.BlockSpec((tb, d), lambda i, j: (i, 0)),
                  pl.BlockSpec((1, d), lambda i, j: (0, 0))],
        out_specs=pl.BlockSpec((tb, d), lambda i, j: (i, 0)),
        out_shape=jax.ShapeDtypeStruct((t, d), F32),
        scratch_shapes=([pltpu.VMEM((eb, tsub), F32)] * (tb // tsub)
                        + [pltpu.VMEM((eb // 2, tsub), jnp.uint32)] * (tb // tsub)
                        + [pltpu.VMEM((d, tsub), F32)] * (tb // tsub)),
        compiler_params=_params("parallel", "arbitrary"),
        name="peer_ffn",
    )(xn, u, u, vt, cnt, rank, pp, qq, h, norm_w.reshape(1, d).astype(F32))


def _pair_order(v_tab):
    ne, d = v_tab.shape
    sub = SUBLANES
    return v_tab.reshape(ne // (2 * sub), 2, sub, d).transpose(0, 2, 1, 3).reshape(ne, d)


def peer_layer(h, norm_w, w_q, sub_keys, u_tab, v_tab, out_norm_w, *, final_norm):
    q, xn = norm_matmul(h, norm_w, w_q.astype(BF16), emit_xn=True)
    route = peer_route(q, sub_keys.astype(BF16))
    vt = _pair_order(v_tab.astype(BF16)).T
    return peer_ffn(xn, u_tab.astype(BF16), vt, route, h, out_norm_w, final_norm=final_norm)


def kernel(x, norm_mix_w, norm_ffn_w, norm_out_w, ev_in_w, ev_out_w, hgrn_lb, hgrn_gn_w, s5_a_re, s5_a_im, s5_log_dt, s5_b_re, s5_b_im, s5_c_re, s5_c_im, s5_d, s5_glu_w, od_in_w, od_out_w, gdn_conv_w, gdn_a_log, gdn_dt_bias, gdn_gn_w, peer_wq, peer_sub_keys, peer_u, peer_v):
    bsz, seq, d = x.shape
    depth = norm_mix_w.shape[0]
    t = bsz * seq
    h = x.reshape(t, d).astype(F32)
    lb_all = jnp.cumsum(jax.nn.softmax(hgrn_lb.astype(F32), axis=0), axis=0)
    hgrn_w = HGRN_HEADS * HGRN_D
    for layer in range(depth):
        j = layer // 2
        if layer % 2 == 0:
            proj = norm_matmul(h, norm_mix_w[layer], ev_in_w[j].astype(BF16))
            o_a = hgrn2(proj, lb_all[j], hgrn_gn_w[j], bsz, seq)
            weights = _s5_weights(s5_a_re[j], s5_a_im[j], s5_log_dt[j], s5_b_re[j], s5_b_im[j],
                                  s5_c_re[j], s5_c_im[j], s5_d[j])
            y4 = s5(proj, 4 * hgrn_w, weights, bsz, seq)
            w_out = ev_out_w[j].astype(BF16)
            h = even_out(h, o_a, y4, s5_glu_w[j].astype(BF16), w_out[:hgrn_w], w_out[hgrn_w:])
        else:
            w_in = od_in_w[j]
            pad = (-w_in.shape[1]) % LANES
            w_in = jnp.pad(w_in, ((0, 0), (0, pad))).astype(BF16)
            proj = norm_matmul(h, norm_mix_w[layer], w_in)
            o_c = gdn(proj, gdn_conv_w[j], gdn_a_log[j], gdn_dt_bias[j], gdn_gn_w[j], bsz, seq)
            h = matmul_residual(h, o_c, od_out_w[j].astype(BF16))
        h = peer_layer(h, norm_ffn_w[layer], peer_wq[layer], peer_sub_keys[layer], peer_u[layer],
                       peer_v[layer], norm_out_w, final_norm=(layer == depth - 1))
    return h.reshape(bsz, seq, d)
```
